```python
import jax, jax.numpy as jnp
from jax import lax
import numpy as np

D_MODEL = 1024
BATCH = 8
SEQ = 2048
DEPTH = 4
DEC_BATCH = 128
DEC_SEQ = 4
PAST_LEN = 8192
PAGE_SIZE = 128

HEAD_DIM = 64
ROPE_THETA = 10000.0
EPS = 1e-6
ATTN_SCALE = HEAD_DIM ** -0.5
D_FF = 2816
MOBA_HEADS = 8
MOBA_KV_HEADS = 4
MOBA_GROUP = MOBA_HEADS // MOBA_KV_HEADS
MOBA_BLOCK = 256
MOBA_TOPK = 3
MOBA_QCHUNK = 16
MLA_HEADS = 8
MLA_Q_LORA = 256
MLA_KV_LORA = 256
MLA_NOPE = 64
MLA_ROPE = 32
MLA_V = 64
MLA_SCALE = (MLA_NOPE + MLA_ROPE) ** -0.5
MLA_QBLOCK = 128
DIL_CONFIGS = ((128, 1), (512, 4), (2048, 16))
N_DIL = len(DIL_CONFIGS)
DIL_HEADS = 8
DIL_QBLOCK = 128
N_EVEN = (DEPTH + 1) // 2
N_ODD = DEPTH // 2
EVEN_SPLITS = (MOBA_HEADS * HEAD_DIM, MOBA_KV_HEADS * HEAD_DIM, MOBA_KV_HEADS * HEAD_DIM, MLA_Q_LORA, MLA_KV_LORA, MLA_ROPE)
EVEN_IN = sum(EVEN_SPLITS)
EVEN_OUT = MOBA_HEADS * HEAD_DIM + MLA_HEADS * MLA_V
ODD_IN = N_DIL * 3 * DIL_HEADS * HEAD_DIM
ODD_OUT = DIL_HEADS * HEAD_DIM

kernel_name = 'hybrid_moba_mla_dilated_step'


def rms_norm(x, g):
    x32 = x.astype(jnp.float32)
    y = x32 * lax.rsqrt(jnp.mean(x32 * x32, axis=-1, keepdims=True) + EPS)
    return (y * g.astype(jnp.float32)).astype(x.dtype)


def rope(x, pos):
    half = x.shape[-1] // 2
    inv = ROPE_THETA ** (-jnp.arange(half, dtype=jnp.float32) / half)
    ang = pos.astype(jnp.float32)[:, None] * inv[None, :]
    shp = (pos.shape[0],) + (1,) * (x.ndim - 3) + (half,)
    cos, sin = jnp.cos(ang).reshape(shp), jnp.sin(ang).reshape(shp)
    x32 = x.astype(jnp.float32)
    x1, x2 = x32[..., :half], x32[..., half:]
    return jnp.concatenate([x1 * cos - x2 * sin, x2 * cos + x1 * sin], axis=-1).astype(x.dtype)


def swiglu(h, w_in, w_out):
    g, u = jnp.split(h @ w_in, 2, axis=-1)
    return (jax.nn.silu(g) * u) @ w_out


def query_sweep(fn, qs, qpos, chunk):
    n_q = qpos.shape[0]
    n = n_q // chunk
    blocks = tuple(jnp.swapaxes(a.reshape((a.shape[0], n, chunk) + a.shape[2:]), 0, 1) for a in qs)
    out = lax.map(lambda args: fn(*args), blocks + (qpos.reshape(n, chunk),))
    out = jnp.swapaxes(out, 0, 1)
    return out.reshape((out.shape[0], n_q) + out.shape[3:])


def moba_blocks(k_ctx, v_ctx):
    b, l = k_ctx.shape[:2]
    nb = -(-l // MOBA_BLOCK)
    pad = ((0, 0), (0, nb * MOBA_BLOCK - l), (0, 0), (0, 0))
    kb = jnp.pad(k_ctx, pad).reshape(b, nb, MOBA_BLOCK, MOBA_KV_HEADS, HEAD_DIM).transpose(0, 3, 1, 2, 4)
    vb = jnp.pad(v_ctx, pad).reshape(b, nb, MOBA_BLOCK, MOBA_KV_HEADS, HEAD_DIM).transpose(0, 3, 1, 2, 4)
    kmean = jnp.mean(kb.astype(jnp.float32), axis=3)
    return kb, vb, kmean


def moba_attend(q, qpos, kb, vb, kmean):
    b, n_q = q.shape[:2]
    nb = kb.shape[2]
    qg = q.reshape(b, n_q, MOBA_KV_HEADS, MOBA_GROUP, HEAD_DIM).transpose(0, 2, 3, 1, 4)
    own = (qpos // MOBA_BLOCK).astype(jnp.int32)
    gate = jnp.einsum('bkgqd,bknd->bkgqn', qg.astype(jnp.float32), kmean)
    gate = jnp.where(jnp.arange(nb)[None, :] < own[:, None], gate, -jnp.inf)
    top_v, top_i = lax.top_k(gate, min(MOBA_TOPK, nb))
    sel = jnp.concatenate([top_i.astype(jnp.int32), jnp.broadcast_to(own[:, None], top_i.shape[:-1] + (1,))], axis=-1)
    slot_ok = jnp.concatenate([jnp.isfinite(top_v), jnp.ones(top_v.shape[:-1] + (1,), dtype=bool)], axis=-1)
    bi = jnp.arange(b)[:, None, None, None, None]
    hi = jnp.arange(MOBA_KV_HEADS)[None, :, None, None, None]
    ks = kb[bi, hi, sel]
    vs = vb[bi, hi, sel]
    kpos = sel[..., None] * MOBA_BLOCK + jnp.arange(MOBA_BLOCK, dtype=jnp.int32)
    mask = slot_ok[..., None] & (kpos <= qpos[:, None, None])
    s = jnp.einsum('bkgqd,bkgqnjd->bkgqnj', qg, ks).astype(jnp.float32) * ATTN_SCALE
    s = jnp.where(mask, s, -jnp.inf)
    n_keys = sel.shape[-1] * MOBA_BLOCK
    p = jax.nn.softmax(s.reshape(s.shape[:4] + (n_keys,)), axis=-1)
    o = jnp.einsum('bkgqm,bkgqmd->bkgqd', p.astype(vs.dtype), vs.reshape(vs.shape[:4] + (n_keys, HEAD_DIM)))
    return o.transpose(0, 3, 1, 2, 4).reshape(b, n_q, MOBA_HEADS, HEAD_DIM)


def mla_attend(q_lat, q_rope, qpos, ckv, kr):
    l = ckv.shape[1]
    s = (jnp.einsum('bqhc,blc->bhql', q_lat, ckv) + jnp.einsum('bqhr,blr->bhql', q_rope, kr)).astype(jnp.float32) * MLA_SCALE
    s = jnp.where(jnp.arange(l)[None, :] <= qpos[:, None], s, -jnp.inf)
    p = jax.nn.softmax(s, axis=-1)
    return jnp.einsum('bhql,blc->bqhc', p.astype(ckv.dtype), ckv)


def even_project(h, pos, w_in, moba_q_g, moba_k_g, mla_cq_g, mla_w_uq, mla_q_g, mla_ckv_g, mla_kr_g, mla_w_uk):
    b, l, _ = h.shape
    qa, ka, va, cq, ckv, kr = jnp.split(h @ w_in, np.cumsum(EVEN_SPLITS[:-1]).tolist(), axis=-1)
    qa = rope(rms_norm(qa.reshape(b, l, MOBA_HEADS, HEAD_DIM), moba_q_g), pos)
    ka = rope(rms_norm(ka.reshape(b, l, MOBA_KV_HEADS, HEAD_DIM), moba_k_g), pos)
    va = va.reshape(b, l, MOBA_KV_HEADS, HEAD_DIM)
    q = (rms_norm(cq, mla_cq_g) @ mla_w_uq).reshape(b, l, MLA_HEADS, MLA_NOPE + MLA_ROPE)
    q = rms_norm(q, mla_q_g)
    q_lat = jnp.einsum('blhn,chn->blhc', q[..., :MLA_NOPE], mla_w_uk)
    q_rope = rope(q[..., MLA_NOPE:], pos)
    ckv = rms_norm(ckv, mla_ckv_g)
    kr = rope(rms_norm(kr, mla_kr_g)[:, :, None, :], pos)[:, :, 0, :]
    return qa, ka, va, q_lat, q_rope, ckv, kr


def even_output(oa, o_lat, mla_w_uv, w_out):
    b, l = oa.shape[:2]
    ob = jnp.einsum('blhc,chv->blhv', o_lat, mla_w_uv)
    return jnp.concatenate([oa.reshape(b, l, -1), ob.reshape(b, l, -1)], axis=-1) @ w_out


def odd_project(h, pos, w_in, q_g, k_g):
    b, l, _ = h.shape
    proj = (h @ w_in).reshape(b, l, N_DIL, 3, DIL_HEADS, HEAD_DIM)
    q = rope(rms_norm(proj[:, :, :, 0], q_g[:, None, :]), pos)
    k = rope(rms_norm(proj[:, :, :, 1], k_g[:, None, :]), pos)
    return q, k, proj[:, :, :, 2]


def dilated_branch(q, qidx, k_ctx, v_ctx, dil, n_keys):
    kidx = qidx[:, None] - dil * jnp.arange(n_keys, dtype=jnp.int32)[None, :]
    valid = kidx >= 0
    kidx = jnp.maximum(kidx, 0)
    kg = k_ctx[:, kidx]
    vg = v_ctx[:, kidx]
    s = jnp.einsum('bqhd,bqnhd->bqhn', q, kg).astype(jnp.float32) * ATTN_SCALE
    s = jnp.where(valid[None, :, None, :], s, -jnp.inf)
    lse = jax.nn.logsumexp(s, axis=-1)
    p = jnp.exp(s - lse[..., None])
    o = jnp.einsum('bqhn,bqnhd->bqhd', p.astype(vg.dtype), vg)
    return o, lse


def dilated_combine(q, qidxs, ctxs):
    outs, lses = [], []
    for g, (w, d) in enumerate(DIL_CONFIGS):
        o, l = dilated_branch(q[:, :, g], qidxs[g], ctxs[g][0], ctxs[g][1], d, w // d + 1)
        outs.append(o)
        lses.append(l)
    wts = jax.nn.softmax(jnp.stack(lses, axis=0), axis=0)
    return jnp.einsum('gbqh,gbqhd->bqhd', wts, jnp.stack(outs, axis=0).astype(jnp.float32)).astype(q.dtype)


def setup_inputs(seed: int = 0) -> dict:
    key = jax.random.key(seed)
    ks = iter(jax.random.split(key, 40))
    f32 = jnp.float32

    def nrm(shape, scale=1.0):
        return scale * jax.random.normal(next(ks), shape, f32)

    def gain(shape):
        return 1.0 + 0.02 * jax.random.normal(next(ks), shape, f32)

    n_pages = PAST_LEN // PAGE_SIZE
    n_pool = (DEC_BATCH * n_pages * 5) // 4
    page_table = jax.random.permutation(next(ks), n_pool)[:DEC_BATCH * n_pages].reshape(DEC_BATCH, n_pages).astype(jnp.int32)
    dil_states = [nrm((N_ODD, DEC_BATCH, min(w, PAST_LEN), 2, DIL_HEADS, HEAD_DIM)) for w, _ in DIL_CONFIGS]
    return {
        'x_prompt': nrm((BATCH, SEQ, D_MODEL)),
        'x_sample': nrm((DEC_BATCH, DEC_SEQ, D_MODEL)),
        'cache_moba_kv': nrm((N_EVEN, n_pool, PAGE_SIZE, 2, MOBA_KV_HEADS, HEAD_DIM)),
        'cache_mla': nrm((N_EVEN, n_pool, PAGE_SIZE, MLA_KV_LORA + MLA_ROPE)),
        'state_dil128': dil_states[0],
        'state_dil512': dil_states[1],
        'state_dil2048': dil_states[2],
        'page_table': page_table,
        'norm_g': gain((DEPTH, 3, D_MODEL)),
        'ffn_w_in': nrm((DEPTH, 2, D_MODEL, 2 * D_FF), D_MODEL ** -0.5),
        'ffn_w_out': nrm((DEPTH, 2, D_FF, D_MODEL), D_FF ** -0.5),
        'even_w_in': nrm((N_EVEN, D_MODEL, EVEN_IN), D_MODEL ** -0.5),
        'moba_q_g': gain((N_EVEN, HEAD_DIM)),
        'moba_k_g': gain((N_EVEN, HEAD_DIM)),
        'mla_cq_g': gain((N_EVEN, MLA_Q_LORA)),
        'mla_w_uq': nrm((N_EVEN, MLA_Q_LORA, MLA_HEADS * (MLA_NOPE + MLA_ROPE)), MLA_Q_LORA ** -0.5),
        'mla_q_g': gain((N_EVEN, MLA_NOPE + MLA_ROPE)),
        'mla_ckv_g': gain((N_EVEN, MLA_KV_LORA)),
        'mla_kr_g': gain((N_EVEN, MLA_ROPE)),
        'mla_w_uk': nrm((N_EVEN, MLA_KV_LORA, MLA_HEADS, MLA_NOPE), MLA_KV_LORA ** -0.5),
        'mla_w_uv': nrm((N_EVEN, MLA_KV_LORA, MLA_HEADS, MLA_V), MLA_KV_LORA ** -0.5),
        'even_w_out': nrm((N_EVEN, EVEN_OUT, D_MODEL), EVEN_OUT ** -0.5),
        'odd_w_in': nrm((N_ODD, D_MODEL, ODD_IN), D_MODEL ** -0.5),
        'dil_q_g': gain((N_ODD, N_DIL, HEAD_DIM)),
        'dil_k_g': gain((N_ODD, N_DIL, HEAD_DIM)),
        'odd_w_out': nrm((N_ODD, ODD_OUT, D_MODEL), ODD_OUT ** -0.5),
    }


def reference(x_prompt, x_sample, cache_moba_kv, cache_mla, state_dil128, state_dil512, state_dil2048, page_table,
              norm_g, ffn_w_in, ffn_w_out, even_w_in, moba_q_g, moba_k_g, mla_cq_g, mla_w_uq, mla_q_g, mla_ckv_g,
              mla_kr_g, mla_w_uk, mla_w_uv, even_w_out, odd_w_in, dil_q_g, dil_k_g, odd_w_out):
    b_p, s_p = x_prompt.shape[:2]
    b_s, t_s = x_sample.shape[:2]
    past = page_table.shape[1] * cache_moba_kv.shape[2]
    pos_p = jnp.arange(s_p, dtype=jnp.int32)
    pos_s = past + jnp.arange(t_s, dtype=jnp.int32)
    state_dils = (state_dil128, state_dil512, state_dil2048)
    xp, xs = x_prompt, x_sample
    moba_p, moba_s, mla_p, mla_s = [], [], [], []
    dil_p = [[] for _ in DIL_CONFIGS]
    dil_s = [[] for _ in DIL_CONFIGS]
    for layer in range(DEPTH):
        li = layer // 2
        xp = xp + 0.5 * swiglu(rms_norm(xp, norm_g[layer, 0]), ffn_w_in[layer, 0], ffn_w_out[layer, 0])
        xs = xs + 0.5 * swiglu(rms_norm(xs, norm_g[layer, 0]), ffn_w_in[layer, 0], ffn_w_out[layer, 0])
        hp = rms_norm(xp, norm_g[layer, 1])
        hs = rms_norm(xs, norm_g[layer, 1])
        if layer % 2 == 0:
            ep = (even_w_in[li], moba_q_g[li], moba_k_g[li], mla_cq_g[li], mla_w_uq[li], mla_q_g[li],
                  mla_ckv_g[li], mla_kr_g[li], mla_w_uk[li])
            qa, ka, va, ql, qr, ckv, kr = even_project(hp, pos_p, *ep)
            kb, vb, km = moba_blocks(ka, va)
            oa = query_sweep(lambda q, qp: moba_attend(q, qp, kb, vb, km), (qa,), pos_p, MOBA_QCHUNK)
            ol = query_sweep(lambda a, c, qp: mla_attend(a, c, qp, ckv, kr), (ql, qr), pos_p, MLA_QBLOCK)
            yp = even_output(oa, ol, mla_w_uv[li], even_w_out[li])
            moba_p.append(jnp.stack([ka, va], axis=2))
            mla_p.append(jnp.concatenate([ckv, kr], axis=-1))
            qa, ka, va, ql, qr, ckv, kr = even_project(hs, pos_s, *ep)
            past_kv = cache_moba_kv[li, page_table].reshape(b_s, past, 2, MOBA_KV_HEADS, HEAD_DIM)
            kb, vb, km = moba_blocks(jnp.concatenate([past_kv[:, :, 0], ka], axis=1),
                                     jnp.concatenate([past_kv[:, :, 1], va], axis=1))
            oa = query_sweep(lambda q, qp: moba_attend(q, qp, kb, vb, km), (qa,), pos_s, 1)
            past_lat = cache_mla[li, page_table].reshape(b_s, past, MLA_KV_LORA + MLA_ROPE)
            ol = mla_attend(ql, qr, pos_s, jnp.concatenate([past_lat[..., :MLA_KV_LORA], ckv], axis=1),
                            jnp.concatenate([past_lat[..., MLA_KV_LORA:], kr], axis=1))
            ys = even_output(oa, ol, mla_w_uv[li], even_w_out[li])
            moba_s.append(jnp.stack([ka, va], axis=2))
            mla_s.append(jnp.concatenate([ckv, kr], axis=-1))
        else:
            q, k, v = odd_project(hp, pos_p, odd_w_in[li], dil_q_g[li], dil_k_g[li])
            ctxs = [(k[:, :, g], v[:, :, g]) for g in range(N_DIL)]
            o = query_sweep(lambda qb, qp: dilated_combine(qb, (qp,) * N_DIL, ctxs), (q,), pos_p, DIL_QBLOCK)
            yp = o.reshape(b_p, s_p, ODD_OUT) @ odd_w_out[li]
            for g, (w, _) in enumerate(DIL_CONFIGS):
                wp = min(w, s_p)
                dil_p[g].append(jnp.stack([k[:, s_p - wp:, g], v[:, s_p - wp:, g]], axis=2))
            q, k, v = odd_project(hs, pos_s, odd_w_in[li], dil_q_g[li], dil_k_g[li])
            ctxs, qidxs = [], []
            for g, (w, _) in enumerate(DIL_CONFIGS):
                buf = state_dils[g][li]
                nbuf = buf.shape[1]
                kc = jnp.concatenate([buf[:, :, 0], k[:, :, g]], axis=1)
                vc = jnp.concatenate([buf[:, :, 1], v[:, :, g]], axis=1)
                ctxs.append((kc, vc))
                qidxs.append(nbuf + jnp.arange(t_s, dtype=jnp.int32))
                keep = min(w, nbuf + t_s)
                dil_s[g].append(jnp.stack([kc[:, nbuf + t_s - keep:], vc[:, nbuf + t_s - keep:]], axis=2))
            o = dilated_combine(q, tuple(qidxs), ctxs)
            ys = o.reshape(b_s, t_s, ODD_OUT) @ odd_w_out[li]
        xp = xp + yp
        xs = xs + ys
        xp = xp + 0.5 * swiglu(rms_norm(xp, norm_g[layer, 2]), ffn_w_in[layer, 1], ffn_w_out[layer, 1])
        xs = xs + 0.5 * swiglu(rms_norm(xs, norm_g[layer, 2]), ffn_w_in[layer, 1], ffn_w_out[layer, 1])
    return (xp, xs,
            jnp.stack(moba_p, axis=0), jnp.stack(moba_s, axis=0),
            jnp.stack(mla_p, axis=0), jnp.stack(mla_s, axis=0),
            jnp.stack(dil_p[0], axis=0), jnp.stack(dil_s[0], axis=0),
            jnp.stack(dil_p[1], axis=0), jnp.stack(dil_s[1], axis=0),
            jnp.stack(dil_p[2], axis=0), jnp.stack(dil_s[2], axis=0))
```

```python
import functools

import numpy as np
import jax
import jax.numpy as jnp
from jax import lax
from jax.experimental import pallas as pl
from jax.experimental.pallas import tpu as pltpu

F32 = jnp.float32
BF16 = jnp.bfloat16

HEAD_DIM = 64
ROPE_THETA = 10000.0
EPS = 1e-6
ATTN_SCALE = HEAD_DIM ** -0.5
MOBA_HEADS = 8
MOBA_KV_HEADS = 4
MOBA_BLOCK = 256
MOBA_TOPK = 3
MLA_HEADS = 8
MLA_Q_LORA = 256
MLA_KV_LORA = 256
MLA_NOPE = 64
MLA_ROPE = 32
MLA_V = 64
MLA_SCALE = (MLA_NOPE + MLA_ROPE) ** -0.5
MLA_QW = 384
DIL_CONFIGS = ((128, 1), (512, 4), (2048, 16))
DIL_HEADS = 8
DIL_TILE = 128
LANES = 128
PAGES_PER_CHUNK = 16
V7X_VMEM_BYTES = 64 * 1024 * 1024


def _cparams(sem, vmem_mb):
    assert vmem_mb * 1024 * 1024 < V7X_VMEM_BYTES
    return pltpu.CompilerParams(dimension_semantics=sem, vmem_limit_bytes=vmem_mb * 1024 * 1024)


def _nt(a, b):
    return lax.dot_general(a, b, (((1,), (1,)), ((), ())), preferred_element_type=F32)


def _dot(a, b):
    return jnp.dot(a, b, preferred_element_type=F32)


def _split(a):
    hi = a.astype(BF16)
    lo = (a - hi.astype(F32)).astype(BF16)
    return hi, lo


def _seg_sum(a, m):
    hi, lo = _split(a)
    return _dot(hi, m) + _dot(lo, m)


def _nt_hp(a, b):
    ah, al = _split(a)
    bh, bl = _split(b)
    return _nt(ah, bh) + _nt(ah, bl) + _nt(al, bh)


def _rms(x, g):
    return x * lax.rsqrt(jnp.mean(x * x, axis=-1, keepdims=True) + EPS) * g


def _rope(x, cos, sin, half):
    c = x.shape[-1]
    lane = lax.broadcasted_iota(jnp.int32, x.shape, 1)
    first = (lane & (2 * half - 1)) < half
    partner = jnp.where(first, pltpu.roll(x, c - half, 1), pltpu.roll(x, half, 1))
    return x * cos + partner * sin


def _ffn_kernel(x_ref, g_ref, wg_ref, wu_ref, wo_ref, o_ref, xn_ref, acc_ref):
    j = pl.program_id(1)

    @pl.when(j == 0)
    def _():
        xn_ref[...] = _rms(x_ref[...], g_ref[...]).astype(BF16)
        acc_ref[...] = jnp.zeros_like(acc_ref)

    xn = xn_ref[...]
    gate = _dot(xn, wg_ref[...])
    up = _dot(xn, wu_ref[...])
    h = (gate * jax.nn.sigmoid(gate) * up).astype(BF16)
    acc_ref[...] += _dot(h, wo_ref[...])

    @pl.when(j == pl.num_programs(1) - 1)
    def _():
        o_ref[...] = x_ref[...] + 0.5 * acc_ref[...]


def _ffn(x, g, w_in, w_out, tm=512, tf=1408):
    t, d = x.shape
    tm = min(tm, t)
    dff = w_out.shape[0]
    nf = dff // tf
    assert t % tm == 0 and dff % tf == 0
    return pl.pallas_call(
        _ffn_kernel,
        out_shape=jax.ShapeDtypeStruct((t, d), F32),
        grid=(t // tm, nf),
        in_specs=[
            pl.BlockSpec((tm, d), lambda i, j: (i, 0)),
            pl.BlockSpec((1, d), lambda i, j: (0, 0)),
            pl.BlockSpec((d, tf), lambda i, j: (0, j)),
            pl.BlockSpec((d, tf), lambda i, j: (0, j + nf)),
            pl.BlockSpec((tf, d), lambda i, j: (j, 0)),
        ],
        out_specs=pl.BlockSpec((tm, d), lambda i, j: (i, 0)),
        scratch_shapes=[pltpu.VMEM((tm, d), BF16), pltpu.VMEM((tm, d), F32)],
        compiler_params=_cparams(("parallel", "arbitrary"), 48),
        name="ffn",
    )(x, g.reshape(1, d), w_in, w_in, w_out)


def _even_proj_kernel(x_ref, g_ref, w_ref, seg_ref, m96_ref, wuq_ref, wuk_ref, place_ref,
                      qg_ref, kg_ref, cqg_ref, mqg_ref, ckvg_ref, krg_ref,
                      c64_ref, s64_ref, c32_ref, s32_ref,
                      qa_ref, kv_ref, lat_ref, qf_ref, latk_ref):
    xn = _rms(x_ref[...], g_ref[...]).astype(BF16)
    proj = _dot(xn, w_ref[...])
    qa, ka, va = proj[:, 0:512], proj[:, 512:768], proj[:, 768:1024]
    cq, ckv, kr = proj[:, 1024:1280], proj[:, 1280:1536], proj[:, 1536:1664]
    seg = seg_ref[...]
    c64, s64, c32, s32 = c64_ref[...], s64_ref[...], c32_ref[...], s32_ref[...]
    inv_hd = 1.0 / HEAD_DIM

    qa_n = qa * lax.rsqrt(_seg_sum(qa * qa, seg) * inv_hd + EPS) * qg_ref[...]
    qa_ref[...] = _rope(qa_n, c64, s64, HEAD_DIM // 2)
    ka_n = ka * lax.rsqrt(_seg_sum(ka * ka, seg[:256, :256]) * inv_hd + EPS) * kg_ref[...]
    kv_ref[:, 0:256] = _rope(ka_n, c64[:, :256], s64[:, :256], HEAD_DIM // 2)
    kv_ref[:, 256:512] = va

    cq_n = _rms(cq, cqg_ref[...]).astype(BF16)
    q = _dot(cq_n, wuq_ref[...])
    q_n = q * lax.rsqrt(_seg_sum(q * q, m96_ref[...]) * (1.0 / (MLA_NOPE + MLA_ROPE)) + EPS) * mqg_ref[...]
    q_lat = _dot(q_n[:, :512].astype(BF16), wuk_ref[...])
    q_rope = _rope(q_n[:, 512:], c32, s32, MLA_ROPE // 2).astype(BF16)
    tails = _dot(q_rope, place_ref[...])
    for h in range(MLA_HEADS):
        qf_ref[:, h * MLA_QW:h * MLA_QW + 256] = q_lat[:, h * 256:(h + 1) * 256].astype(BF16)
        qf_ref[:, h * MLA_QW + 256:(h + 1) * MLA_QW] = tails[:, h * LANES:(h + 1) * LANES].astype(BF16)

    ckv_n = _rms(ckv, ckvg_ref[...])
    kr_n = kr * lax.rsqrt(jnp.sum(kr * kr, axis=-1, keepdims=True) * (1.0 / MLA_ROPE) + EPS) * krg_ref[...]
    kr_r = _rope(kr_n, c32[:, :LANES], s32[:, :LANES], MLA_ROPE // 2)
    lat_ref[:, 0:256] = ckv_n
    lat_ref[:, 256:288] = kr_r[:, :MLA_ROPE]
    latk_ref[:, 0:256] = ckv_n.astype(BF16)
    latk_ref[:, 256:384] = kr_r.astype(BF16)


def _even_proj(x, tabs, wts, tm=256):
    t, d = x.shape
    tm = min(tm, t)
    c64, s64, c32, s32 = tabs
    ntab = c64.shape[0] // tm
    const = lambda a: pl.BlockSpec(a.shape, lambda i: (0,) * a.ndim)
    tab = lambda a: pl.BlockSpec((tm, a.shape[1]), lambda i: (i % ntab, 0))
    row = lambda n: pl.BlockSpec((tm, n), lambda i: (i, 0))
    names = ("g", "w_in", "seg", "m96", "wuq", "wuk", "place", "qg", "kg", "cqg", "mqg", "ckvg", "krg")
    consts = [wts[n] for n in names]
    return pl.pallas_call(
        _even_proj_kernel,
        out_shape=(
            jax.ShapeDtypeStruct((t, 512), F32),
            jax.ShapeDtypeStruct((t, 512), F32),
            jax.ShapeDtypeStruct((t, 288), F32),
            jax.ShapeDtypeStruct((t, MLA_HEADS * MLA_QW), BF16),
            jax.ShapeDtypeStruct((t, MLA_QW), BF16),
        ),
        grid=(t // tm,),
        in_specs=[row(d)] + [const(a) for a in consts] + [tab(c64), tab(s64), tab(c32), tab(s32)],
        out_specs=(row(512), row(512), row(288), row(MLA_HEADS * MLA_QW), row(MLA_QW)),
        compiler_params=_cparams(("parallel",), 48),
        name="even_proj",
    )(x, *consts, c64, s64, c32, s32)


def _moba_prompt_kernel(q_ref, kv_ref, e_ref, o_ref, kb_ref, vb_ref, km_ref, *, nb):
    own = pl.program_id(1)
    tq = q_ref.shape[0]
    nkeys = kv_ref.shape[0]

    @pl.when(own == 0)
    def _():
        kb_ref[...] = kv_ref[:, 0:256].astype(BF16)
        vb_ref[...] = kv_ref[:, 256:512].astype(BF16)
        km_ref[...] = jnp.zeros_like(km_ref)
        for n in range(nb):
            km_ref[n:n + 1, :] = jnp.mean(kv_ref[n * MOBA_BLOCK:(n + 1) * MOBA_BLOCK, 0:256], axis=0, keepdims=True)

    lane = lax.broadcasted_iota(jnp.int32, (tq, LANES), 1)
    past = lane < own
    row = lax.broadcasted_iota(jnp.int32, (tq, nkeys), 0)
    col = lax.broadcasted_iota(jnp.int32, (tq, nkeys), 1)
    lo = own * MOBA_BLOCK
    in_own = col >= lo
    causal = jnp.where(col <= row + lo, 1.0, 0.0)
    for h in range(MOBA_HEADS):
        k = h // (MOBA_HEADS // MOBA_KV_HEADS)
        q_h = q_ref[:, h * HEAD_DIM:(h + 1) * HEAD_DIM]
        gate = jnp.where(past, _nt_hp(q_h, km_ref[:, k * HEAD_DIM:(k + 1) * HEAD_DIM]), -jnp.inf)
        beaten = jnp.zeros((tq, LANES), F32)
        for m in range(nb):
            gm = gate[:, m:m + 1]
            ahead = (gm > gate) | ((gm == gate) & (lane > m))
            beaten = beaten + jnp.where(ahead, 1.0, 0.0)
        sel = jnp.where(past & (beaten < MOBA_TOPK), 1.0, 0.0).astype(BF16)
        allowed = jnp.where(in_own, causal, _dot(sel, e_ref[...]))
        s = _nt((q_h * ATTN_SCALE).astype(BF16), kb_ref[:, k * HEAD_DIM:(k + 1) * HEAD_DIM])
        s = jnp.where(allowed > 0.5, s, -jnp.inf)
        p = jnp.exp(s - jnp.max(s, axis=-1, keepdims=True))
        l = jnp.sum(p, axis=-1, keepdims=True)
        o = _dot(p.astype(BF16), vb_ref[:, k * HEAD_DIM:(k + 1) * HEAD_DIM])
        o_ref[:, h * HEAD_DIM:(h + 1) * HEAD_DIM] = (o / l).astype(o_ref.dtype)


def _moba_prompt(qa, kv, b, l):
    nb = l // MOBA_BLOCK
    assert l % MOBA_BLOCK == 0 and nb <= LANES
    expand = jnp.asarray(np.arange(LANES)[:, None] == (np.arange(l) // MOBA_BLOCK)[None, :], BF16)
    return pl.pallas_call(
        functools.partial(_moba_prompt_kernel, nb=nb),
        out_shape=jax.ShapeDtypeStruct((b * l, 512), BF16),
        grid=(b, nb),
        in_specs=[
            pl.BlockSpec((MOBA_BLOCK, 512), lambda i, j: (i * nb + j, 0)),
            pl.BlockSpec((l, 512), lambda i, j: (i, 0)),
            pl.BlockSpec((LANES, l), lambda i, j: (0, 0)),
        ],
        out_specs=pl.BlockSpec((MOBA_BLOCK, 512), lambda i, j: (i * nb + j, 0)),
        scratch_shapes=[pltpu.VMEM((l, 256), BF16), pltpu.VMEM((l, 256), BF16), pltpu.VMEM((LANES, 256), F32)],
        compiler_params=_cparams(("parallel", "arbitrary"), 48),
        name="moba_prompt",
    )(qa, kv, expand)


def _mla_prompt_kernel(q_ref, k_ref, o_ref, *, tq):
    qi = pl.program_id(1)
    row = lax.broadcasted_iota(jnp.int32, (tq, tq), 0)
    col = lax.broadcasted_iota(jnp.int32, (tq, tq), 1)
    for h in range(MLA_HEADS):
        q = q_ref[:, h * MLA_QW:(h + 1) * MLA_QW]

        def body(c, carry, q=q):
            m, l, acc = carry
            k = k_ref[pl.ds(pl.multiple_of(c * tq, tq), tq), :]
            s = _nt(q, k) * MLA_SCALE
            s = jnp.where(col + c * tq <= row + qi * tq, s, -jnp.inf)
            m_new = jnp.maximum(m, jnp.max(s, axis=-1, keepdims=True))
            alpha = jnp.exp(m - m_new)
            p = jnp.exp(s - m_new)
            l = alpha * l + jnp.sum(p, axis=-1, keepdims=True)
            acc = alpha * acc + _dot(p.astype(BF16), k[:, :MLA_KV_LORA])
            return m_new, l, acc

        init = (jnp.full((tq, 1), -jnp.inf, F32), jnp.zeros((tq, 1), F32), jnp.zeros((tq, MLA_KV_LORA), F32))
        _, l, acc = lax.fori_loop(0, qi + 1, body, init)
        o_ref[:, h * MLA_KV_LORA:(h + 1) * MLA_KV_LORA] = (acc / l).astype(o_ref.dtype)


def _mla_prompt(qf, latk, b, l, tq=256):
    nq = l // tq
    return pl.pallas_call(
        functools.partial(_mla_prompt_kernel, tq=tq),
        out_shape=jax.ShapeDtypeStruct((b * l, MLA_HEADS * MLA_KV_LORA), BF16),
        grid=(b, nq),
        in_specs=[
            pl.BlockSpec((tq, MLA_HEADS * MLA_QW), lambda i, j: (i * nq + j, 0)),
            pl.BlockSpec((l, MLA_QW), lambda i, j: (i, 0)),
        ],
        out_specs=pl.BlockSpec((tq, MLA_HEADS * MLA_KV_LORA), lambda i, j: (i * nq + j, 0)),
        compiler_params=_cparams(("parallel", "arbitrary"), 32),
        name="mla_prompt",
    )(qf, latk)


def _even_sample_kernel(pt_ref, qbd_ref, kvnew_ref, qf_ref, latnew_ref, cm_hbm, cl_hbm,
                        oa_ref, ol_ref,
                        mbuf, lbuf, sem, km_ref, mpart, lpart, opart, newkv, newlat, m_ref, l_ref, acc_ref,
                        *, li, nch, n_tok):
    step = pl.program_id(0)
    nsteps = pl.num_programs(0)
    c = step % nch
    slot = step % 2
    ppc = mbuf.shape[1]
    page = mbuf.shape[2]
    nkeys = ppc * page
    bpc = nkeys // MOBA_BLOCK
    nblk = nch * bpc
    nq = qbd_ref.shape[1]
    heads = nq // n_tok

    def copies(s, sl):
        base = s * ppc
        out = []
        for p in range(ppc):
            pg = pt_ref[base + p]
            out.append(pltpu.make_async_copy(cm_hbm.at[li, pg], mbuf.at[sl, p], sem.at[0, sl]))
            out.append(pltpu.make_async_copy(cl_hbm.at[li, pg], lbuf.at[sl, p], sem.at[1, sl]))
        return out

    @pl.when(step == 0)
    def _():
        for cp in copies(0, 0):
            cp.start()

    @pl.when(step + 1 < nsteps)
    def _():
        for cp in copies(step + 1, 1 - slot):
            cp.start()

    for cp in copies(step, slot):
        cp.wait()

    lane = lax.broadcasted_iota(jnp.int32, (nq, LANES), 1)
    tok = lax.broadcasted_iota(jnp.int32, (nq, LANES), 0) // heads

    @pl.when(c == 0)
    def _():
        m_ref[...] = jnp.full_like(m_ref, -jnp.inf)
        l_ref[...] = jnp.zeros_like(l_ref)
        acc_ref[...] = jnp.zeros_like(acc_ref)
        mpart[...] = jnp.zeros_like(mpart)
        lpart[...] = jnp.zeros_like(lpart)
        km_ref[...] = jnp.zeros_like(km_ref)

    qbd = qbd_ref[0]
    qs = (qbd * ATTN_SCALE).astype(BF16)
    ppb = MOBA_BLOCK // page
    mp = mpart[...]
    lp = lpart[...]
    for n in range(bpc):
        blk = c * bpc + n
        kv = mbuf[slot, n * ppb:(n + 1) * ppb].reshape(MOBA_BLOCK, 512)
        s_n = _nt(qs, kv[:, 0:256].astype(BF16))
        m_n = jnp.max(s_n, axis=-1, keepdims=True)
        p_n = jnp.exp(s_n - m_n)
        mp = jnp.where(lane == blk, m_n, mp)
        lp = jnp.where(lane == blk, jnp.sum(p_n, axis=-1, keepdims=True), lp)
        opart[blk] = _dot(p_n.astype(BF16), kv[:, 256:512].astype(BF16))
        km_ref[pl.ds(blk, 1), :] = jnp.mean(kv[:, 0:256], axis=0, keepdims=True)
    mpart[...] = mp
    lpart[...] = lp

    qf = qf_ref[0]
    lats, scores = [], []
    for n in range(bpc):
        lat = lbuf[slot, n * ppb:(n + 1) * ppb].reshape(MOBA_BLOCK, 288).astype(BF16)
        lats.append(lat[:, 0:256])
        scores.append(_nt(qf[:, 0:256], lat[:, 0:256]) + _nt(qf[:, 256:288], lat[:, 256:288]))
    s = jnp.concatenate(scores, axis=1) * MLA_SCALE
    m_old = m_ref[...]
    m_new = jnp.maximum(m_old, jnp.max(s, axis=-1, keepdims=True))
    alpha = jnp.exp(m_old - m_new)
    p = jnp.exp(s - m_new).astype(BF16)
    l_ref[...] = alpha * l_ref[...] + jnp.sum(p.astype(F32), axis=-1, keepdims=True)
    acc = alpha * acc_ref[...]
    for n in range(bpc):
        acc = acc + _dot(p[:, n * MOBA_BLOCK:(n + 1) * MOBA_BLOCK], lats[n])
    acc_ref[...] = acc
    m_ref[...] = m_new

    @pl.when(c == nch - 1)
    def _():
        new_ok = (lane <= tok) & (lane < n_tok)
        gate = jnp.where(lane < nblk, _nt_hp(qbd, km_ref[...]), -jnp.inf)
        beaten = jnp.zeros((nq, LANES), F32)
        for m in range(nblk):
            gm = gate[:, m:m + 1]
            ahead = (gm > gate) | ((gm == gate) & (lane > m))
            beaten = beaten + jnp.where(ahead, 1.0, 0.0)
        sel = (lane < nblk) & (beaten < MOBA_TOPK)
        newkv[...] = jnp.zeros_like(newkv)
        newkv[0:n_tok, :] = kvnew_ref[0]
        nk = newkv[...]
        s_new = jnp.where(new_ok, _nt(qs, nk[:, 0:256].astype(BF16)), -jnp.inf)
        mp2 = mpart[...]
        m_all = jnp.maximum(jnp.max(jnp.where(sel, mp2, -jnp.inf), axis=-1, keepdims=True),
                            jnp.max(s_new, axis=-1, keepdims=True))
        w = jnp.where(sel, jnp.exp(mp2 - m_all), 0.0)
        p_new = jnp.exp(s_new - m_all)
        l_tot = jnp.sum(w * lpart[...], axis=-1, keepdims=True) + jnp.sum(p_new, axis=-1, keepdims=True)
        o = _dot(p_new.astype(BF16), nk[:, 256:512].astype(BF16))
        for n in range(nblk):
            o = o + w[:, n:n + 1] * opart[n]
        o = o / l_tot
        rowi = lax.broadcasted_iota(jnp.int32, (nq, HEAD_DIM), 0)
        kvh = (rowi % heads) // (MOBA_HEADS // MOBA_KV_HEADS)
        o64 = jnp.zeros((nq, HEAD_DIM), F32)
        for k in range(MOBA_KV_HEADS):
            o64 = o64 + jnp.where(kvh == k, o[:, k * HEAD_DIM:(k + 1) * HEAD_DIM], 0.0)
        oa_ref[0] = o64.astype(oa_ref.dtype)

        newlat[...] = jnp.zeros_like(newlat)
        newlat[0:n_tok, :] = latnew_ref[0]
        nl = newlat[...].astype(BF16)
        s2 = jnp.where(new_ok, _nt(qf, nl) * MLA_SCALE, -jnp.inf)
        m_o = m_ref[...]
        m_f = jnp.maximum(m_o, jnp.max(s2, axis=-1, keepdims=True))
        a_f = jnp.exp(m_o - m_f)
        p2 = jnp.exp(s2 - m_f)
        l_f = a_f * l_ref[...] + jnp.sum(p2, axis=-1, keepdims=True)
        acc = a_f * acc_ref[...] + _dot(p2.astype(BF16), nl[:, 0:256])
        ol_ref[0] = (acc / l_f).astype(ol_ref.dtype)


def _even_sample(page_table, qbd, kvnew, qf, latnew, cache_moba, cache_mla, li):
    b, nq, _ = qbd.shape
    n_tok = kvnew.shape[1]
    n_pages = page_table.shape[1]
    page = cache_moba.shape[2]
    ppc = PAGES_PER_CHUNK
    nch = n_pages // ppc
    assert n_pages % ppc == 0 and (ppc * page) % MOBA_BLOCK == 0 and n_pages * page // MOBA_BLOCK <= LANES
    nblk = n_pages * page // MOBA_BLOCK
    grid_spec = pltpu.PrefetchScalarGridSpec(
        num_scalar_prefetch=1,
        grid=(b * nch,),
        in_specs=[
            pl.BlockSpec((1, nq, 256), lambda s, pt: (s // nch, 0, 0)),
            pl.BlockSpec((1, n_tok, 512), lambda s, pt: (s // nch, 0, 0)),
            pl.BlockSpec((1, nq, MLA_QW), lambda s, pt: (s // nch, 0, 0)),
            pl.BlockSpec((1, n_tok, MLA_QW), lambda s, pt: (s // nch, 0, 0)),
            pl.BlockSpec(memory_space=pl.ANY),
            pl.BlockSpec(memory_space=pl.ANY),
        ],
        out_specs=(
            pl.BlockSpec((1, nq, HEAD_DIM), lambda s, pt: (s // nch, 0, 0)),
            pl.BlockSpec((1, nq, MLA_KV_LORA), lambda s, pt: (s // nch, 0, 0)),
        ),
        scratch_shapes=[
            pltpu.VMEM((2, ppc, page, 512), F32),
            pltpu.VMEM((2, ppc, page, 288), F32),
            pltpu.SemaphoreType.DMA((2, 2)),
            pltpu.VMEM((LANES, 256), F32),
            pltpu.VMEM((nq, LANES), F32),
            pltpu.VMEM((nq, LANES), F32),
            pltpu.VMEM((nblk, nq, 256), F32),
            pltpu.VMEM((LANES, 512), F32),
            pltpu.VMEM((LANES, MLA_QW), F32),
            pltpu.VMEM((nq, 1), F32),
            pltpu.VMEM((nq, 1), F32),
            pltpu.VMEM((nq, MLA_KV_LORA), F32),
        ],
    )
    return pl.pallas_call(
        functools.partial(_even_sample_kernel, li=li, nch=nch, n_tok=n_tok),
        out_shape=(jax.ShapeDtypeStruct((b, nq, HEAD_DIM), BF16), jax.ShapeDtypeStruct((b, nq, MLA_KV_LORA), BF16)),
        grid_spec=grid_spec,
        compiler_params=_cparams(("arbitrary",), 48),
        name="even_sample",
    )(page_table.reshape(-1), qbd, kvnew, qf, latnew, cache_moba, cache_mla)


def _even_out_kernel(x_ref, oa_ref, ol_ref, wuv_ref, wo_ref, o_ref):
    ob = _dot(ol_ref[...], wuv_ref[...]).astype(BF16)
    y = _dot(oa_ref[...], wo_ref[0:512, :]) + _dot(ob, wo_ref[512:1024, :])
    o_ref[...] = x_ref[...] + y


def _even_out(x, oa, ol, wuv_bd, w_out, tm=512):
    t, d = x.shape
    tm = min(tm, t)
    row = lambda n: pl.BlockSpec((tm, n), lambda i: (i, 0))
    const = lambda a: pl.BlockSpec(a.shape, lambda i: (0,) * a.ndim)
    return pl.pallas_call(
        _even_out_kernel,
        out_shape=jax.ShapeDtypeStruct((t, d), F32),
        grid=(t // tm,),
        in_specs=[row(d), row(512), row(MLA_HEADS * MLA_KV_LORA), const(wuv_bd), const(w_out)],
        out_specs=row(d),
        compiler_params=_cparams(("parallel",), 40),
        name="even_out",
    )(x, oa, ol, wuv_bd, w_out)


def _odd_proj_kernel(x_ref, g_ref, w_ref, seg_ref, qg_ref, kg_ref, c64_ref, s64_ref,
                     q_ref, kv0_ref, kv1_ref, kv2_ref, kb0_ref, kb1_ref, kb2_ref):
    xn = _rms(x_ref[...], g_ref[...]).astype(BF16)
    seg = seg_ref[...]
    c64, s64 = c64_ref[...], s64_ref[...]
    inv_hd = 1.0 / HEAD_DIM
    for g, (kv_ref, kb_ref) in enumerate(((kv0_ref, kb0_ref), (kv1_ref, kb1_ref), (kv2_ref, kb2_ref))):
        proj = _dot(xn, w_ref[:, g * 1536:(g + 1) * 1536])
        q, k, v = proj[:, 0:512], proj[:, 512:1024], proj[:, 1024:1536]
        q_n = q * lax.rsqrt(_seg_sum(q * q, seg) * inv_hd + EPS) * qg_ref[g:g + 1, :]
        q_ref[:, g * 512:(g + 1) * 512] = (_rope(q_n, c64, s64, HEAD_DIM // 2) * ATTN_SCALE).astype(BF16)
        k_n = k * lax.rsqrt(_seg_sum(k * k, seg) * inv_hd + EPS) * kg_ref[g:g + 1, :]
        k_r = _rope(k_n, c64, s64, HEAD_DIM // 2)
        kv_ref[:, 0:512] = k_r
        kv_ref[:, 512:1024] = v
        kb_ref[:, 0:512] = k_r.astype(BF16)
        kb_ref[:, 512:1024] = v.astype(BF16)


def _odd_proj(x, tabs, wts, tm=256):
    t, d = x.shape
    tm = min(tm, t)
    c64, s64 = tabs
    ntab = c64.shape[0] // tm
    const = lambda a: pl.BlockSpec(a.shape, lambda i: (0,) * a.ndim)
    tab = lambda a: pl.BlockSpec((tm, a.shape[1]), lambda i: (i % ntab, 0))
    row = lambda n: pl.BlockSpec((tm, n), lambda i: (i, 0))
    consts = [wts[n] for n in ("g", "w_in", "seg", "qg", "kg")]
    return pl.pallas_call(
        _odd_proj_kernel,
        out_shape=(jax.ShapeDtypeStruct((t, 1536), BF16),) + (jax.ShapeDtypeStruct((t, 1024), F32),) * 3
        + (jax.ShapeDtypeStruct((t, 1024), BF16),) * 3,
        grid=(t // tm,),
        in_specs=[row(d)] + [const(a) for a in consts] + [tab(c64), tab(s64)],
        out_specs=(row(1536),) + (row(1024),) * 6,
        compiler_params=_cparams(("parallel",), 48),
        name="odd_proj",
    )(x, *consts, c64, s64)


def _dil_prompt_kernel(q_ref, kp_ref, kc_ref, vp_ref, vc_ref, o_ref, lse_ref):
    i = pl.program_id(2)
    t = q_ref.shape[1]
    row = lax.broadcasted_iota(jnp.int32, (t, 2 * t), 0)
    col = lax.broadcasted_iota(jnp.int32, (t, 2 * t), 1)
    allowed = (col >= row) & (col <= row + t) & ((col >= t) | (i > 0))
    q = q_ref[0]
    k = jnp.concatenate([kp_ref[0], kc_ref[0]], axis=0)
    v = jnp.concatenate([vp_ref[0], vc_ref[0]], axis=0)
    for h in range(DIL_HEADS):
        hs = slice(h * HEAD_DIM, (h + 1) * HEAD_DIM)
        s = jnp.where(allowed, _nt(q[:, hs], k[:, hs]), -jnp.inf)
        m = jnp.max(s, axis=-1, keepdims=True)
        p = jnp.exp(s - m)
        l = jnp.sum(p, axis=-1, keepdims=True)
        o_ref[0, :, hs] = _dot(p.astype(BF16), v[:, hs]) / l
        lse_ref[0, :, hs] = jnp.broadcast_to(m + jnp.log(l), (t, HEAD_DIM))


def _dil_prompt(q, kvb, g, dil, b, l):
    t = DIL_TILE
    ls = l // dil
    ni = ls // t
    assert l % dil == 0 and ls % t == 0
    qv = q.reshape(b, ls, dil * 1536)
    kv = kvb.reshape(b, ls, dil * 1024)
    blk = lambda f: pl.BlockSpec((1, t, 512), f)
    out = pl.pallas_call(
        _dil_prompt_kernel,
        out_shape=(jax.ShapeDtypeStruct((b, ls, dil * 512), F32),) * 2,
        grid=(b, dil, ni),
        in_specs=[
            blk(lambda bi, r, i: (bi, i, r * 3 + g)),
            blk(lambda bi, r, i: (bi, jnp.maximum(i - 1, 0), r * 2)),
            blk(lambda bi, r, i: (bi, i, r * 2)),
            blk(lambda bi, r, i: (bi, jnp.maximum(i - 1, 0), r * 2 + 1)),
            blk(lambda bi, r, i: (bi, i, r * 2 + 1)),
        ],
        out_specs=(blk(lambda bi, r, i: (bi, i, r)),) * 2,
        compiler_params=_cparams(("parallel", "parallel", "arbitrary"), 32),
        name=f"dil_prompt_{g}",
    )(qv, kv, kv, kv, kv)
    return out[0].reshape(b * l, 512), out[1].reshape(b * l, 512)


def _dil_sample_kernel(*refs, dil, n_tok, aliased):
    if aliased:
        qbd_ref, new_ref, st_ref, _, nst_ref, o_ref, lse_ref, newpad = refs
    else:
        qbd_ref, new_ref, st_ref, nst_ref, o_ref, lse_ref, newpad = refs
    w = st_ref.shape[0]
    nq = qbd_ref.shape[1]
    heads = nq // n_tok
    new = new_ref[0]
    nst_ref[0:w - n_tok, :] = st_ref[n_tok:w, :]
    nst_ref[w - n_tok:w, :] = new

    q = qbd_ref[0]
    kb = st_ref[:, 0:512].astype(BF16)
    vb = st_ref[:, 512:1024].astype(BF16)
    tok = lax.broadcasted_iota(jnp.int32, (nq, w), 0) // heads
    r = lax.broadcasted_iota(jnp.int32, (nq, w), 1)
    ok = (r >= tok) & (((w + tok - r) & (dil - 1)) == 0)
    s = jnp.where(ok, _nt(q, kb), -jnp.inf)

    newpad[...] = jnp.zeros_like(newpad)
    newpad[0:n_tok, :] = new
    npd = newpad[...]
    tok2 = lax.broadcasted_iota(jnp.int32, (nq, LANES), 0) // heads
    j = lax.broadcasted_iota(jnp.int32, (nq, LANES), 1)
    ok2 = (j <= tok2) & (j < n_tok) & (((tok2 - j) & (dil - 1)) == 0)
    s2 = jnp.where(ok2, _nt(q, npd[:, 0:512].astype(BF16)), -jnp.inf)

    m = jnp.maximum(jnp.max(s, axis=-1, keepdims=True), jnp.max(s2, axis=-1, keepdims=True))
    p = jnp.exp(s - m)
    p2 = jnp.exp(s2 - m)
    l = jnp.sum(p, axis=-1, keepdims=True) + jnp.sum(p2, axis=-1, keepdims=True)
    o = (_dot(p.astype(BF16), vb) + _dot(p2.astype(BF16), npd[:, 512:1024].astype(BF16))) / l
    head = lax.broadcasted_iota(jnp.int32, (nq, HEAD_DIM), 0) % heads
    o64 = jnp.zeros((nq, HEAD_DIM), F32)
    for h in range(DIL_HEADS):
        o64 = o64 + jnp.where(head == h, o[:, h * HEAD_DIM:(h + 1) * HEAD_DIM], 0.0)
    o_ref[0] = o64
    lse_ref[0] = jnp.broadcast_to(m + jnp.log(l), (nq, HEAD_DIM))


def _dil_sample(qbd, new, state, prev_out, li, dil):
    n_odd, b, w, _ = state.shape
    nq = qbd.shape[1]
    n_tok = new.shape[1]
    aliased = prev_out is not None
    in_specs = [
        pl.BlockSpec((1, nq, 512), lambda i: (i, 0, 0)),
        pl.BlockSpec((1, n_tok, 1024), lambda i: (i, 0, 0)),
        pl.BlockSpec((None, None, w, 1024), lambda i: (li, i, 0, 0)),
    ]
    args = [qbd, new, state]
    if aliased:
        in_specs.append(pl.BlockSpec(memory_space=pl.ANY))
        args.append(prev_out)
    return pl.pallas_call(
        functools.partial(_dil_sample_kernel, dil=dil, n_tok=n_tok, aliased=aliased),
        out_shape=(
            jax.ShapeDtypeStruct(state.shape, F32),
            jax.ShapeDtypeStruct((b, nq, HEAD_DIM), F32),
            jax.ShapeDtypeStruct((b, nq, HEAD_DIM), F32),
        ),
        grid=(b,),
        in_specs=in_specs,
        out_specs=(
            pl.BlockSpec((None, None, w, 1024), lambda i: (li, i, 0, 0)),
            pl.BlockSpec((1, nq, HEAD_DIM), lambda i: (i, 0, 0)),
            pl.BlockSpec((1, nq, HEAD_DIM), lambda i: (i, 0, 0)),
        ),
        scratch_shapes=[pltpu.VMEM((LANES, 1024), F32)],
        input_output_aliases={3: 0} if aliased else {},
        compiler_params=_cparams(("parallel",), 56),
        name=f"dil_sample_{dil}",
    )(*args)


def _odd_out_kernel(x_ref, o0_ref, o1_ref, o2_ref, l0_ref, l1_ref, l2_ref, wo_ref, out_ref):
    l0, l1, l2 = l0_ref[...], l1_ref[...], l2_ref[...]
    m = jnp.maximum(jnp.maximum(l0, l1), l2)
    e0, e1, e2 = jnp.exp(l0 - m), jnp.exp(l1 - m), jnp.exp(l2 - m)
    o = (e0 * o0_ref[...] + e1 * o1_ref[...] + e2 * o2_ref[...]) / (e0 + e1 + e2)
    out_ref[...] = x_ref[...] + _dot(o.astype(BF16), wo_ref[...])


def _odd_out(x, outs, lses, w_out, tm=512):
    t, d = x.shape
    tm = min(tm, t)
    row = lambda n: pl.BlockSpec((tm, n), lambda i: (i, 0))
    return pl.pallas_call(
        _odd_out_kernel,
        out_shape=jax.ShapeDtypeStruct((t, d), F32),
        grid=(t // tm,),
        in_specs=[row(d)] + [row(512)] * 6 + [pl.BlockSpec(w_out.shape, lambda i: (0, 0))],
        out_specs=row(d),
        compiler_params=_cparams(("parallel",), 40),
        name="odd_out",
    )(x, *outs, *lses, w_out)


def _rope_tables(pos, dim, reps):
    half = dim // 2
    inv = ROPE_THETA ** (-jnp.arange(half, dtype=F32) / half)
    ang = pos.astype(F32)[:, None] * inv[None, :]
    c, s = jnp.cos(ang), jnp.sin(ang)
    return jnp.tile(jnp.concatenate([c, c], axis=-1), (1, reps)), jnp.tile(jnp.concatenate([-s, s], axis=-1), (1, reps))


def _seg_matrix(group_of_lane):
    g = np.asarray(group_of_lane)
    return jnp.asarray(g[:, None] == g[None, :], BF16)


def _block_diag_q(q, n_heads, lanes_of_head):
    b, t, _ = q.shape
    n_groups = max(lanes_of_head) + 1
    onehot = jnp.asarray(np.asarray(lanes_of_head)[:, None] == np.arange(n_groups)[None, :], q.dtype)
    qh = q.reshape(b, t, n_heads, 1, HEAD_DIM) * onehot[None, None, :, :, None]
    return qh.reshape(b, t * n_heads, n_groups * HEAD_DIM)


def kernel(x_prompt, x_sample, cache_moba_kv, cache_mla, state_dil128, state_dil512, state_dil2048, page_table,
           norm_g, ffn_w_in, ffn_w_out, even_w_in, moba_q_g, moba_k_g, mla_cq_g, mla_w_uq, mla_q_g, mla_ckv_g,
           mla_kr_g, mla_w_uk, mla_w_uv, even_w_out, odd_w_in, dil_q_g, dil_k_g, odd_w_out):
    bp, sp, d = x_prompt.shape
    bs, ts, _ = x_sample.shape
    depth = norm_g.shape[0]
    n_even, n_pool, page = cache_moba_kv.shape[:3]
    past = page_table.shape[1] * page
    pos_p = jnp.arange(sp, dtype=jnp.int32)
    pos_s = past + jnp.arange(ts, dtype=jnp.int32)
    tm_s = min(256, bs * ts)
    pos_s_tile = jnp.tile(pos_s, tm_s // ts)
    tab64_p, tab64_s = _rope_tables(pos_p, HEAD_DIM, 8), _rope_tables(pos_s_tile, HEAD_DIM, 8)
    tab32_p, tab32_s = _rope_tables(pos_p, MLA_ROPE, 8), _rope_tables(pos_s_tile, MLA_ROPE, 8)

    seg64 = _seg_matrix(np.arange(512) // HEAD_DIM)
    qdim = MLA_NOPE + MLA_ROPE
    perm = np.concatenate([np.concatenate([np.arange(h * qdim, h * qdim + MLA_NOPE) for h in range(MLA_HEADS)]),
                           np.concatenate([np.arange(h * qdim + MLA_NOPE, (h + 1) * qdim) for h in range(MLA_HEADS)])])
    m96 = _seg_matrix(perm // qdim)
    place = jnp.asarray((np.arange(256)[:, None] // MLA_ROPE == np.arange(1024)[None, :] // LANES)
                        & (np.arange(256)[:, None] % MLA_ROPE == np.arange(1024)[None, :] % LANES), BF16)
    eye_h = np.eye(MLA_HEADS, dtype=np.float32)

    cache_moba = cache_moba_kv.reshape(n_even, n_pool, page, 2 * MOBA_KV_HEADS * HEAD_DIM)
    states = [s.reshape(s.shape[0], s.shape[1], s.shape[2], 2 * DIL_HEADS * HEAD_DIM)
              for s in (state_dil128, state_dil512, state_dil2048)]
    new_states = [None, None, None]

    xp = x_prompt.reshape(bp * sp, d)
    xs = x_sample.reshape(bs * ts, d)
    moba_p, moba_s, mla_p, mla_s = [], [], [], []
    dil_p = [[] for _ in DIL_CONFIGS]

    for layer in range(depth):
        li = layer // 2
        w_in = ffn_w_in[layer, 0].astype(BF16)
        w_out = ffn_w_out[layer, 0].astype(BF16)
        xp = _ffn(xp, norm_g[layer, 0], w_in, w_out)
        xs = _ffn(xs, norm_g[layer, 0], w_in, w_out)
        if layer % 2 == 0:
            wts = {
                "g": norm_g[layer, 1].reshape(1, d),
                "w_in": jnp.pad(even_w_in[li], ((0, 0), (0, 1664 - even_w_in.shape[2]))).astype(BF16),
                "seg": seg64,
                "m96": m96,
                "wuq": mla_w_uq[li][:, perm].astype(BF16),
                "wuk": (jnp.transpose(mla_w_uk[li], (1, 2, 0))[:, :, None, :] * eye_h[:, None, :, None]
                        ).reshape(MLA_HEADS * MLA_NOPE, MLA_HEADS * MLA_KV_LORA).astype(BF16),
                "place": place,
                "qg": jnp.tile(moba_q_g[li], MOBA_HEADS).reshape(1, 512),
                "kg": jnp.tile(moba_k_g[li], MOBA_KV_HEADS).reshape(1, 256),
                "cqg": mla_cq_g[li].reshape(1, MLA_Q_LORA),
                "mqg": jnp.tile(mla_q_g[li], MLA_HEADS)[perm].reshape(1, MLA_HEADS * qdim),
                "ckvg": mla_ckv_g[li].reshape(1, MLA_KV_LORA),
                "krg": jnp.pad(mla_kr_g[li], (0, LANES - MLA_ROPE)).reshape(1, LANES),
            }
            wuv_bd = (jnp.transpose(mla_w_uv[li], (1, 0, 2))[:, :, None, :] * eye_h[:, None, :, None]
                      ).reshape(MLA_HEADS * MLA_KV_LORA, MLA_HEADS * MLA_V).astype(BF16)
            w_o = even_w_out[li].astype(BF16)
            qa, kv, lat, qf, latk = _even_proj(xp, tab64_p + tab32_p, wts)
            oa = _moba_prompt(qa, kv, bp, sp)
            ol = _mla_prompt(qf, latk, bp, sp)
            xp = _even_out(xp, oa, ol, wuv_bd, w_o)
            moba_p.append(kv.reshape(bp, sp, 2, MOBA_KV_HEADS, HEAD_DIM))
            mla_p.append(lat.reshape(bp, sp, MLA_KV_LORA + MLA_ROPE))
            qa, kv, lat, qf, latk = _even_proj(xs, tab64_s + tab32_s, wts)
            kvh_of_head = [h // (MOBA_HEADS // MOBA_KV_HEADS) for h in range(MOBA_HEADS)]
            qbd = _block_diag_q(qa.reshape(bs, ts, 512), MOBA_HEADS, kvh_of_head)
            oa, ol = _even_sample(page_table, qbd, kv.reshape(bs, ts, 512), qf.reshape(bs, ts * MLA_HEADS, MLA_QW),
                                  latk.astype(F32).reshape(bs, ts, MLA_QW), cache_moba, cache_mla, li)
            xs = _even_out(xs, oa.reshape(bs * ts, 512), ol.reshape(bs * ts, MLA_HEADS * MLA_KV_LORA), wuv_bd, w_o)
            moba_s.append(kv.reshape(bs, ts, 2, MOBA_KV_HEADS, HEAD_DIM))
            mla_s.append(lat.reshape(bs, ts, MLA_KV_LORA + MLA_ROPE))
        else:
            wts = {
                "g": norm_g[layer, 1].reshape(1, d),
                "w_in": odd_w_in[li].astype(BF16),
                "seg": seg64,
                "qg": jnp.tile(dil_q_g[li], (1, DIL_HEADS)),
                "kg": jnp.tile(dil_k_g[li], (1, DIL_HEADS)),
            }
            w_o = odd_w_out[li].astype(BF16)
            q, kv0, kv1, kv2, kb0, kb1, kb2 = _odd_proj(xp, tab64_p, wts)
            outs, lses = [], []
            for g, ((w, dil), kvf, kvb) in enumerate(zip(DIL_CONFIGS, (kv0, kv1, kv2), (kb0, kb1, kb2))):
                o, lse = _dil_prompt(q, kvb, g, dil, bp, sp)
                outs.append(o)
                lses.append(lse)
                wp = min(w, sp)
                dil_p[g].append(kvf.reshape(bp, sp, 2, DIL_HEADS, HEAD_DIM)[:, sp - wp:])
            xp = _odd_out(xp, outs, lses, w_o)
            q, kv0, kv1, kv2, _, _, _ = _odd_proj(xs, tab64_s, wts)
            outs, lses = [], []
            for g, ((w, dil), kvf) in enumerate(zip(DIL_CONFIGS, (kv0, kv1, kv2))):
                qbd = _block_diag_q(q[:, g * 512:(g + 1) * 512].reshape(bs, ts, 512), DIL_HEADS, list(range(DIL_HEADS)))
                nst, o, lse = _dil_sample(qbd, kvf.reshape(bs, ts, 1024), states[g], new_states[g], li, dil)
                new_states[g] = nst
                outs.append(o.reshape(bs * ts, 512))
                lses.append(lse.reshape(bs * ts, 512))
            xs = _odd_out(xs, outs, lses, w_o)
        w_in = ffn_w_in[layer, 1].astype(BF16)
        w_out = ffn_w_out[layer, 1].astype(BF16)
        xp = _ffn(xp, norm_g[layer, 2], w_in, w_out)
        xs = _ffn(xs, norm_g[layer, 2], w_in, w_out)

    dil_s = [ns.reshape(st.shape[0], st.shape[1], st.shape[2], 2, DIL_HEADS, HEAD_DIM)
             for ns, st in zip(new_states, states)]
    return (xp.reshape(bp, sp, d), xs.reshape(bs, ts, d),
            jnp.stack(moba_p, axis=0), jnp.stack(moba_s, axis=0),
            jnp.stack(mla_p, axis=0), jnp.stack(mla_s, axis=0),
            jnp.stack(dil_p[0], axis=0), dil_s[0],
            jnp.stack(dil_p[1], axis=0), dil_s[1],
            jnp.stack(dil_p[2], axis=0), dil_s[2])
```

```python
import functools

import numpy as np
import jax
import jax.numpy as jnp
from jax import lax
from jax.experimental import pallas as pl
from jax.experimental.pallas import tpu as pltpu

F32 = jnp.float32
BF16 = jnp.bfloat16

HEAD_DIM = 64
ROPE_THETA = 10000.0
EPS = 1e-6
ATTN_SCALE = HEAD_DIM ** -0.5
MOBA_HEADS = 8
MOBA_KV_HEADS = 4
MOBA_BLOCK = 256
MOBA_TOPK = 3
MLA_HEADS = 8
MLA_Q_LORA = 256
MLA_KV_LORA = 256
MLA_NOPE = 64
MLA_ROPE = 32
MLA_V = 64
MLA_SCALE = (MLA_NOPE + MLA_ROPE) ** -0.5
MLA_QW = 384
DIL_CONFIGS = ((128, 1), (512, 4), (2048, 16))
DIL_HEADS = 8
DIL_TILE = 128
LANES = 128
PAGES_PER_CHUNK = 16
V7X_VMEM_BYTES = 64 * 1024 * 1024


def _cparams(sem, vmem_mb):
    assert vmem_mb * 1024 * 1024 < V7X_VMEM_BYTES
    return pltpu.CompilerParams(dimension_semantics=sem, vmem_limit_bytes=vmem_mb * 1024 * 1024)


def _nt(a, b):
    return lax.dot_general(a, b, (((1,), (1,)), ((), ())), preferred_element_type=F32)


def _dot(a, b):
    return jnp.dot(a, b, preferred_element_type=F32)


def _split(a):
    hi = a.astype(BF16)
    lo = (a - hi.astype(F32)).astype(BF16)
    return hi, lo


def _seg_sum(a, m):
    hi, lo = _split(a)
    return _dot(hi, m) + _dot(lo, m)


def _nt_hp(a, b):
    ah, al = _split(a)
    bh, bl = _split(b)
    return _nt(ah, bh) + _nt(ah, bl) + _nt(al, bh)


def _lanes(x, n):
    return x if n == LANES else jnp.concatenate([x] * (n // LANES), axis=1)


def _dot_hp(a, b):
    ah, al = _split(a)
    bh, bl = _split(b)
    return _dot(ah, bh) + _dot(ah, bl) + _dot(al, bh)


def _rms(x, g):
    return x * lax.rsqrt(jnp.mean(x * x, axis=-1, keepdims=True) + EPS) * g


def _rope(x, cos, sin, half):
    c = x.shape[-1]
    lane = lax.broadcasted_iota(jnp.int32, x.shape, 1)
    first = (lane & (2 * half - 1)) < half
    partner = jnp.where(first, pltpu.roll(x, c - half, 1), pltpu.roll(x, half, 1))
    return x * cos + partner * sin


def _ffn_kernel(x_ref, g_ref, wg_ref, wu_ref, wo_ref, o_ref, xn_ref, acc_ref):
    j = pl.program_id(1)

    @pl.when(j == 0)
    def _():
        xn_ref[...] = _rms(x_ref[...], g_ref[...]).astype(BF16)
        acc_ref[...] = jnp.zeros_like(acc_ref)

    xn = xn_ref[...]
    gate = _dot(xn, wg_ref[...])
    up = _dot(xn, wu_ref[...])
    h = (gate * jax.nn.sigmoid(gate) * up).astype(BF16)
    acc_ref[...] += _dot(h, wo_ref[...])

    @pl.when(j == pl.num_programs(1) - 1)
    def _():
        o_ref[...] = x_ref[...] + 0.5 * acc_ref[...]


def _ffn(x, g, w_in, w_out, tm=512, tf=1408):
    t, d = x.shape
    tm = min(tm, t)
    dff = w_out.shape[0]
    nf = dff // tf
    assert t % tm == 0 and dff % tf == 0
    return pl.pallas_call(
        _ffn_kernel,
        out_shape=jax.ShapeDtypeStruct((t, d), F32),
        grid=(t // tm, nf),
        in_specs=[
            pl.BlockSpec((tm, d), lambda i, j: (i, 0)),
            pl.BlockSpec((1, d), lambda i, j: (0, 0)),
            pl.BlockSpec((d, tf), lambda i, j: (0, j)),
            pl.BlockSpec((d, tf), lambda i, j: (0, j + nf)),
            pl.BlockSpec((tf, d), lambda i, j: (j, 0)),
        ],
        out_specs=pl.BlockSpec((tm, d), lambda i, j: (i, 0)),
        scratch_shapes=[pltpu.VMEM((tm, d), BF16), pltpu.VMEM((tm, d), F32)],
        compiler_params=_cparams(("parallel", "arbitrary"), 48),
        name="ffn",
    )(x, g.reshape(1, d), w_in, w_in, w_out)


def _even_proj_kernel(x_ref, g_ref, w_ref, seg_ref, m96_ref, wuq_ref, wuk_ref, place_ref,
                      qg_ref, kg_ref, cqg_ref, mqg_ref, ckvg_ref, krg_ref,
                      c64_ref, s64_ref, c32_ref, s32_ref,
                      qa_ref, kv_ref, lat_ref, qf_ref, latk_ref):
    xn = _rms(x_ref[...], g_ref[...]).astype(BF16)
    proj = _dot(xn, w_ref[...])
    qa, ka, va = proj[:, 0:512], proj[:, 512:768], proj[:, 768:1024]
    cq, ckv, kr = proj[:, 1024:1280], proj[:, 1280:1536], proj[:, 1536:1664]
    seg = seg_ref[...]
    c64, s64, c32, s32 = c64_ref[...], s64_ref[...], c32_ref[...], s32_ref[...]
    inv_hd = 1.0 / HEAD_DIM

    qa_n = qa * lax.rsqrt(_seg_sum(qa * qa, seg) * inv_hd + EPS) * qg_ref[...]
    qa_ref[...] = _rope(qa_n, c64, s64, HEAD_DIM // 2)
    ka_n = ka * lax.rsqrt(_seg_sum(ka * ka, seg[:256, :256]) * inv_hd + EPS) * kg_ref[...]
    kv_ref[:, 0:256] = _rope(ka_n, c64[:, :256], s64[:, :256], HEAD_DIM // 2)
    kv_ref[:, 256:512] = va

    cq_n = _rms(cq, cqg_ref[...]).astype(BF16)
    q = _dot(cq_n, wuq_ref[...])
    q_n = q * lax.rsqrt(_seg_sum(q * q, m96_ref[...]) * (1.0 / (MLA_NOPE + MLA_ROPE)) + EPS) * mqg_ref[...]
    q_lat = _dot(q_n[:, :512].astype(BF16), wuk_ref[...])
    q_rope = _rope(q_n[:, 512:], c32, s32, MLA_ROPE // 2).astype(BF16)
    tails = _dot(q_rope, place_ref[...])
    for h in range(MLA_HEADS):
        qf_ref[:, h * MLA_QW:h * MLA_QW + 256] = q_lat[:, h * 256:(h + 1) * 256].astype(BF16)
        qf_ref[:, h * MLA_QW + 256:(h + 1) * MLA_QW] = tails[:, h * LANES:(h + 1) * LANES].astype(BF16)

    ckv_n = _rms(ckv, ckvg_ref[...])
    kr_n = kr * lax.rsqrt(jnp.sum(kr * kr, axis=-1, keepdims=True) * (1.0 / MLA_ROPE) + EPS) * krg_ref[...]
    kr_r = _rope(kr_n, c32[:, :LANES], s32[:, :LANES], MLA_ROPE // 2)
    lat_ref[:, 0:256] = ckv_n
    lat_ref[:, 256:288] = kr_r[:, :MLA_ROPE]
    latk_ref[:, 0:256] = ckv_n.astype(BF16)
    latk_ref[:, 256:384] = kr_r.astype(BF16)


def _even_proj(x, tabs, wts, tm=256):
    t, d = x.shape
    tm = min(tm, t)
    c64, s64, c32, s32 = tabs
    ntab = c64.shape[0] // tm
    const = lambda a: pl.BlockSpec(a.shape, lambda i: (0,) * a.ndim)
    tab = lambda a: pl.BlockSpec((tm, a.shape[1]), lambda i: (i % ntab, 0))
    row = lambda n: pl.BlockSpec((tm, n), lambda i: (i, 0))
    names = ("g", "w_in", "seg", "m96", "wuq", "wuk", "place", "qg", "kg", "cqg", "mqg", "ckvg", "krg")
    consts = [wts[n] for n in names]
    return pl.pallas_call(
        _even_proj_kernel,
        out_shape=(
            jax.ShapeDtypeStruct((t, 512), F32),
            jax.ShapeDtypeStruct((t, 512), F32),
            jax.ShapeDtypeStruct((t, 288), F32),
            jax.ShapeDtypeStruct((t, MLA_HEADS * MLA_QW), BF16),
            jax.ShapeDtypeStruct((t, MLA_QW), BF16),
        ),
        grid=(t // tm,),
        in_specs=[row(d)] + [const(a) for a in consts] + [tab(c64), tab(s64), tab(c32), tab(s32)],
        out_specs=(row(512), row(512), row(288), row(MLA_HEADS * MLA_QW), row(MLA_QW)),
        compiler_params=_cparams(("parallel",), 48),
        name="even_proj",
    )(x, *consts, c64, s64, c32, s32)


def _moba_prompt_kernel(q_ref, kv_ref, o_ref, kb_ref, vb_ref, km_ref, *, nb):
    own = pl.program_id(1)
    tq = q_ref.shape[0]
    grp = MOBA_HEADS // MOBA_KV_HEADS
    rows = grp * tq

    @pl.when(own == 0)
    def _():
        for k in range(MOBA_KV_HEADS):
            kb_ref[k] = kv_ref[:, k * HEAD_DIM:(k + 1) * HEAD_DIM].astype(BF16)
            vb_ref[k] = kv_ref[:, 256 + k * HEAD_DIM:256 + (k + 1) * HEAD_DIM].astype(BF16)
        km_ref[...] = jnp.zeros_like(km_ref)
        for n in range(nb):
            km_ref[n:n + 1, :] = jnp.mean(kv_ref[n * MOBA_BLOCK:(n + 1) * MOBA_BLOCK, 0:256], axis=0, keepdims=True)

    lane = lax.broadcasted_iota(jnp.int32, (rows, LANES), 1)
    past = lane < own
    causal = (lax.broadcasted_iota(jnp.int32, (rows, MOBA_BLOCK), 1)
              <= (lax.broadcasted_iota(jnp.int32, (rows, MOBA_BLOCK), 0) & (tq - 1)))
    own_rows = pl.ds(pl.multiple_of(own * MOBA_BLOCK, MOBA_BLOCK), MOBA_BLOCK)
    for k in range(MOBA_KV_HEADS):
        q2 = jnp.concatenate([q_ref[:, (k * grp + g) * HEAD_DIM:(k * grp + g + 1) * HEAD_DIM] for g in range(grp)],
                             axis=0)
        gate = jnp.where(past, _nt_hp(q2, km_ref[:, k * HEAD_DIM:(k + 1) * HEAD_DIM]), -jnp.inf)
        beaten = jnp.zeros((rows, LANES), F32)
        for m in range(nb):
            gm = gate[:, m:m + 1]
            ahead = (gm > gate) | ((gm == gate) & (lane > m))
            beaten = beaten + jnp.where(ahead, 1.0, 0.0)
        sel = jnp.where(past & (beaten < MOBA_TOPK), 1.0, 0.0)
        qs = (q2 * ATTN_SCALE).astype(BF16)
        s = jnp.where(causal, _nt(qs, kb_ref[k, own_rows, :]), -jnp.inf)
        m0 = jnp.broadcast_to(jnp.max(s, axis=-1, keepdims=True), (rows, LANES))
        p = jnp.exp(s - _lanes(m0, MOBA_BLOCK))
        init = (m0, jnp.broadcast_to(jnp.sum(p, axis=-1, keepdims=True), (rows, LANES)),
                _dot(p.astype(BF16), vb_ref[k, own_rows, :]))

        def body(n, carry, k=k, qs=qs, sel=sel):
            m, l, acc = carry
            blk = pl.ds(pl.multiple_of(n * MOBA_BLOCK, MOBA_BLOCK), MOBA_BLOCK)
            chosen = jnp.sum(jnp.where(lane == n, sel, 0.0), axis=-1, keepdims=True) > 0.5
            s = jnp.where(chosen, _nt(qs, kb_ref[k, blk, :]), -jnp.inf)
            m_new = jnp.maximum(m, jnp.max(s, axis=-1, keepdims=True))
            alpha = jnp.exp(m - m_new)
            p = jnp.exp(s - _lanes(m_new, MOBA_BLOCK))
            return (m_new, alpha * l + jnp.sum(p, axis=-1, keepdims=True),
                    alpha[:, :HEAD_DIM] * acc + _dot(p.astype(BF16), vb_ref[k, blk, :]))

        _, l, acc = lax.fori_loop(0, own, body, init)
        o = acc / l[:, :HEAD_DIM]
        for g in range(grp):
            h = k * grp + g
            o_ref[:, h * HEAD_DIM:(h + 1) * HEAD_DIM] = o[g * tq:(g + 1) * tq, :].astype(o_ref.dtype)


def _moba_prompt(qa, kv, b, l):
    nb = l // MOBA_BLOCK
    assert l % MOBA_BLOCK == 0 and nb <= LANES
    return pl.pallas_call(
        functools.partial(_moba_prompt_kernel, nb=nb),
        out_shape=jax.ShapeDtypeStruct((b * l, 512), BF16),
        grid=(b, nb),
        in_specs=[
            pl.BlockSpec((MOBA_BLOCK, 512), lambda i, j: (i * nb + j, 0)),
            pl.BlockSpec((l, 512), lambda i, j: (i, 0)),
        ],
        out_specs=pl.BlockSpec((MOBA_BLOCK, 512), lambda i, j: (i * nb + j, 0)),
        scratch_shapes=[pltpu.VMEM((MOBA_KV_HEADS, l, HEAD_DIM), BF16), pltpu.VMEM((MOBA_KV_HEADS, l, HEAD_DIM), BF16),
                        pltpu.VMEM((LANES, 256), F32)],
        compiler_params=_cparams(("parallel", "arbitrary"), 48),
        name="moba_prompt",
    )(qa, kv)


def _mla_prompt_kernel(q_ref, k_ref, o_ref, m_ref, l_ref, acc_ref, s_ref, p_ref, *, tq, tk):
    qi = pl.program_id(1)
    rows = MLA_HEADS * tq
    q = jnp.concatenate([q_ref[:, h * MLA_QW:(h + 1) * MLA_QW] for h in range(MLA_HEADS)], axis=0)
    qpos = lax.broadcasted_iota(jnp.int32, (tq, tk), 0) + qi * tq
    col = lax.broadcasted_iota(jnp.int32, (tq, tk), 1)
    m_ref[...] = jnp.full_like(m_ref, -jnp.inf)
    l_ref[...] = jnp.zeros_like(l_ref)
    acc_ref[...] = jnp.zeros_like(acc_ref)

    def body(c, carry):
        k = k_ref[pl.ds(pl.multiple_of(c * tk, tk), tk), :]
        s_ref[...] = _nt(q, k)
        for h in range(MLA_HEADS):
            rs = slice(h * tq, (h + 1) * tq)
            s = jnp.where(col + c * tk <= qpos, s_ref[rs, :] * MLA_SCALE, -jnp.inf)
            m_old = m_ref[rs, :]
            m_new = jnp.maximum(m_old, jnp.max(s, axis=-1, keepdims=True))
            alpha = jnp.exp(m_old - m_new)
            p = jnp.exp(s - _lanes(m_new, tk))
            l_ref[rs, :] = alpha * l_ref[rs, :] + jnp.sum(p, axis=-1, keepdims=True)
            acc_ref[rs, :] = _lanes(alpha, MLA_KV_LORA) * acc_ref[rs, :]
            p_ref[rs, :] = p.astype(BF16)
            m_ref[rs, :] = m_new
        acc_ref[...] += _dot(p_ref[...], k[:, :MLA_KV_LORA])
        return carry

    lax.fori_loop(0, (qi * tq + tq + tk - 1) // tk, body, 0)
    for h in range(MLA_HEADS):
        rs = slice(h * tq, (h + 1) * tq)
        o_ref[:, h * MLA_KV_LORA:(h + 1) * MLA_KV_LORA] = (acc_ref[rs, :] / _lanes(l_ref[rs, :], MLA_KV_LORA)).astype(o_ref.dtype)


def _mla_prompt(qf, latk, b, l, tq=128, tk=256):
    nq = l // tq
    assert l % tq == 0 and l % tk == 0 and tq & (tq - 1) == 0
    rows = MLA_HEADS * tq
    return pl.pallas_call(
        functools.partial(_mla_prompt_kernel, tq=tq, tk=tk),
        out_shape=jax.ShapeDtypeStruct((b * l, MLA_HEADS * MLA_KV_LORA), BF16),
        grid=(b, nq),
        in_specs=[
            pl.BlockSpec((tq, MLA_HEADS * MLA_QW), lambda i, j: (i * nq + j, 0)),
            pl.BlockSpec((l, MLA_QW), lambda i, j: (i, 0)),
        ],
        out_specs=pl.BlockSpec((tq, MLA_HEADS * MLA_KV_LORA), lambda i, j: (i * nq + j, 0)),
        scratch_shapes=[pltpu.VMEM((rows, LANES), F32), pltpu.VMEM((rows, LANES), F32),
                        pltpu.VMEM((rows, MLA_KV_LORA), F32),
                        pltpu.VMEM((rows, tk), F32), pltpu.VMEM((rows, tk), BF16)],
        compiler_params=_cparams(("parallel", "arbitrary"), 32),
        name="mla_prompt",
    )(qf, latk)


def _even_sample_kernel(pt_ref, qbd_ref, kvnew_ref, qf_ref, latnew_ref, cm_hbm, cl_hbm,
                        oa_ref, ol_ref,
                        mbuf, lbuf, sem, km_ref, mpart, lpart, opart, newkv, newlat, m_ref, l_ref, acc_ref,
                        *, li, nch, n_tok):
    step = pl.program_id(0)
    nsteps = pl.num_programs(0)
    c = step % nch
    slot = step % 2
    ppc = mbuf.shape[1]
    page = mbuf.shape[3]
    nkeys = ppc * page
    bpc = nkeys // MOBA_BLOCK
    nblk = nch * bpc
    nq = qbd_ref.shape[1]
    heads = nq // n_tok

    def copies(s, sl):
        base = s * ppc
        out = []
        for p in range(ppc):
            pg = pt_ref[base + p]
            out.append(pltpu.make_async_copy(cm_hbm.at[li, pg], mbuf.at[sl, p], sem.at[0, sl]))
            out.append(pltpu.make_async_copy(cl_hbm.at[li, pg], lbuf.at[sl, p], sem.at[1, sl]))
        return out

    @pl.when(step == 0)
    def _():
        for cp in copies(0, 0):
            cp.start()

    @pl.when(step + 1 < nsteps)
    def _():
        for cp in copies(step + 1, 1 - slot):
            cp.start()

    for cp in copies(step, slot):
        cp.wait()

    lane = lax.broadcasted_iota(jnp.int32, (nq, LANES), 1)
    tok = lax.broadcasted_iota(jnp.int32, (nq, LANES), 0) // heads

    @pl.when(c == 0)
    def _():
        m_ref[...] = jnp.full_like(m_ref, -jnp.inf)
        l_ref[...] = jnp.zeros_like(l_ref)
        acc_ref[...] = jnp.zeros_like(acc_ref)
        mpart[...] = jnp.zeros_like(mpart)
        lpart[...] = jnp.zeros_like(lpart)
        km_ref[...] = jnp.zeros_like(km_ref)

    qbd = qbd_ref[0]
    qs = (qbd * ATTN_SCALE).astype(BF16)
    ppb = MOBA_BLOCK // page
    klane = lax.broadcasted_iota(jnp.int32, (256, LANES), 1)
    scores = [_dot(qs, mbuf[slot, p, 0:256, :].astype(BF16)) for p in range(ppc)]
    mp = mpart[...]
    lp = lpart[...]
    km = km_ref[...]
    for n in range(bpc):
        blk = c * bpc + n
        pages = range(n * ppb, (n + 1) * ppb)
        s_n = jnp.concatenate([scores[p] for p in pages], axis=1)
        m_n = jnp.max(s_n, axis=-1, keepdims=True)
        p_n = jnp.exp(s_n - m_n)
        mp = jnp.where(lane == blk, m_n, mp)
        lp = jnp.where(lane == blk, jnp.sum(p_n, axis=-1, keepdims=True), lp)
        pb = p_n.astype(BF16)
        opart[blk] = sum(_nt(pb[:, i * page:(i + 1) * page], mbuf[slot, p, 256:512, :].astype(BF16))
                         for i, p in enumerate(pages))
        ksum = sum(mbuf[slot, p, 0:256, :] for p in pages)
        km = jnp.where(klane == blk, jnp.sum(ksum, axis=1, keepdims=True) * (1.0 / MOBA_BLOCK), km)
    mpart[...] = mp
    lpart[...] = lp
    km_ref[...] = km

    qf = qf_ref[0]
    lat = [lbuf[slot, p].astype(BF16) for p in range(ppc)]
    s = jnp.concatenate([_dot(qf[:, 0:256], lt[0:256, :]) + _dot(qf[:, 256:288], lt[256:288, :]) for lt in lat],
                        axis=1) * MLA_SCALE
    m_old = m_ref[...]
    m_new = jnp.maximum(m_old, jnp.max(s, axis=-1, keepdims=True))
    alpha = jnp.exp(m_old - m_new)
    p = jnp.exp(s - m_new).astype(BF16)
    l_ref[...] = alpha * l_ref[...] + jnp.sum(p.astype(F32), axis=-1, keepdims=True)
    acc = alpha * acc_ref[...]
    for pg in range(ppc):
        acc = acc + _nt(p[:, pg * page:(pg + 1) * page], lat[pg][0:256, :])
    acc_ref[...] = acc
    m_ref[...] = m_new

    @pl.when(c == nch - 1)
    def _():
        new_ok = (lane <= tok) & (lane < n_tok)
        gate = jnp.where(lane < nblk, _dot_hp(qbd, km_ref[...]), -jnp.inf)
        beaten = jnp.zeros((nq, LANES), F32)
        for m in range(nblk):
            gm = gate[:, m:m + 1]
            ahead = (gm > gate) | ((gm == gate) & (lane > m))
            beaten = beaten + jnp.where(ahead, 1.0, 0.0)
        sel = (lane < nblk) & (beaten < MOBA_TOPK)
        newkv[...] = jnp.zeros_like(newkv)
        newkv[0:n_tok, :] = kvnew_ref[0]
        nk = newkv[...]
        s_new = jnp.where(new_ok, _nt(qs, nk[:, 0:256].astype(BF16)), -jnp.inf)
        mp2 = mpart[...]
        m_all = jnp.maximum(jnp.max(jnp.where(sel, mp2, -jnp.inf), axis=-1, keepdims=True),
                            jnp.max(s_new, axis=-1, keepdims=True))
        w = jnp.where(sel, jnp.exp(mp2 - m_all), 0.0)
        p_new = jnp.exp(s_new - m_all)
        l_tot = jnp.sum(w * lpart[...], axis=-1, keepdims=True) + jnp.sum(p_new, axis=-1, keepdims=True)
        o = _dot(p_new.astype(BF16), nk[:, 256:512].astype(BF16))
        for n in range(nblk):
            o = o + w[:, n:n + 1] * opart[n]
        o = o / l_tot
        rowi = lax.broadcasted_iota(jnp.int32, (nq, HEAD_DIM), 0)
        kvh = (rowi % heads) // (MOBA_HEADS // MOBA_KV_HEADS)
        o64 = jnp.zeros((nq, HEAD_DIM), F32)
        for k in range(MOBA_KV_HEADS):
            o64 = o64 + jnp.where(kvh == k, o[:, k * HEAD_DIM:(k + 1) * HEAD_DIM], 0.0)
        oa_ref[0] = o64.astype(oa_ref.dtype)

        newlat[...] = jnp.zeros_like(newlat)
        newlat[0:n_tok, :] = latnew_ref[0]
        nl = newlat[...].astype(BF16)
        s2 = jnp.where(new_ok, _nt(qf, nl) * MLA_SCALE, -jnp.inf)
        m_o = m_ref[...]
        m_f = jnp.maximum(m_o, jnp.max(s2, axis=-1, keepdims=True))
        a_f = jnp.exp(m_o - m_f)
        p2 = jnp.exp(s2 - m_f)
        l_f = a_f * l_ref[...] + jnp.sum(p2, axis=-1, keepdims=True)
        acc = a_f * acc_ref[...] + _dot(p2.astype(BF16), nl[:, 0:256])
        ol_ref[0] = (acc / l_f).astype(ol_ref.dtype)


def _even_sample(page_table, qbd, kvnew, qf, latnew, cache_moba, cache_mla, li):
    b, nq, _ = qbd.shape
    n_tok = kvnew.shape[1]
    n_pages = page_table.shape[1]
    page = cache_moba.shape[3]
    ppc = PAGES_PER_CHUNK
    nch = n_pages // ppc
    assert n_pages % ppc == 0 and (ppc * page) % MOBA_BLOCK == 0 and n_pages * page // MOBA_BLOCK <= LANES
    nblk = n_pages * page // MOBA_BLOCK
    grid_spec = pltpu.PrefetchScalarGridSpec(
        num_scalar_prefetch=1,
        grid=(b * nch,),
        in_specs=[
            pl.BlockSpec((1, nq, 256), lambda s, pt: (s // nch, 0, 0)),
            pl.BlockSpec((1, n_tok, 512), lambda s, pt: (s // nch, 0, 0)),
            pl.BlockSpec((1, nq, MLA_QW), lambda s, pt: (s // nch, 0, 0)),
            pl.BlockSpec((1, n_tok, MLA_QW), lambda s, pt: (s // nch, 0, 0)),
            pl.BlockSpec(memory_space=pl.ANY),
            pl.BlockSpec(memory_space=pl.ANY),
        ],
        out_specs=(
            pl.BlockSpec((1, nq, HEAD_DIM), lambda s, pt: (s // nch, 0, 0)),
            pl.BlockSpec((1, nq, MLA_KV_LORA), lambda s, pt: (s // nch, 0, 0)),
        ),
        scratch_shapes=[
            pltpu.VMEM((2, ppc, 512, page), F32),
            pltpu.VMEM((2, ppc, 288, page), F32),
            pltpu.SemaphoreType.DMA((2, 2)),
            pltpu.VMEM((256, LANES), F32),
            pltpu.VMEM((nq, LANES), F32),
            pltpu.VMEM((nq, LANES), F32),
            pltpu.VMEM((nblk, nq, 256), F32),
            pltpu.VMEM((LANES, 512), F32),
            pltpu.VMEM((LANES, MLA_QW), F32),
            pltpu.VMEM((nq, 1), F32),
            pltpu.VMEM((nq, 1), F32),
            pltpu.VMEM((nq, MLA_KV_LORA), F32),
        ],
    )
    return pl.pallas_call(
        functools.partial(_even_sample_kernel, li=li, nch=nch, n_tok=n_tok),
        out_shape=(jax.ShapeDtypeStruct((b, nq, HEAD_DIM), BF16), jax.ShapeDtypeStruct((b, nq, MLA_KV_LORA), BF16)),
        grid_spec=grid_spec,
        compiler_params=_cparams(("arbitrary",), 48),
        name="even_sample",
    )(page_table.reshape(-1), qbd, kvnew, qf, latnew, cache_moba, cache_mla)


def _even_out_kernel(x_ref, oa_ref, ol_ref, wuv_ref, wo_ref, o_ref):
    ob = _dot(ol_ref[...], wuv_ref[...]).astype(BF16)
    y = _dot(oa_ref[...], wo_ref[0:512, :]) + _dot(ob, wo_ref[512:1024, :])
    o_ref[...] = x_ref[...] + y


def _even_out(x, oa, ol, wuv_bd, w_out, tm=512):
    t, d = x.shape
    tm = min(tm, t)
    row = lambda n: pl.BlockSpec((tm, n), lambda i: (i, 0))
    const = lambda a: pl.BlockSpec(a.shape, lambda i: (0,) * a.ndim)
    return pl.pallas_call(
        _even_out_kernel,
        out_shape=jax.ShapeDtypeStruct((t, d), F32),
        grid=(t // tm,),
        in_specs=[row(d), row(512), row(MLA_HEADS * MLA_KV_LORA), const(wuv_bd), const(w_out)],
        out_specs=row(d),
        compiler_params=_cparams(("parallel",), 40),
        name="even_out",
    )(x, oa, ol, wuv_bd, w_out)


def _odd_proj_kernel(x_ref, g_ref, w_ref, seg_ref, qg_ref, kg_ref, c64_ref, s64_ref,
                     q_ref, kv0_ref, kv1_ref, kv2_ref, kb0_ref, kb1_ref, kb2_ref):
    xn = _rms(x_ref[...], g_ref[...]).astype(BF16)
    seg = seg_ref[...]
    c64, s64 = c64_ref[...], s64_ref[...]
    inv_hd = 1.0 / HEAD_DIM
    for g, (kv_ref, kb_ref) in enumerate(((kv0_ref, kb0_ref), (kv1_ref, kb1_ref), (kv2_ref, kb2_ref))):
        proj = _dot(xn, w_ref[:, g * 1536:(g + 1) * 1536])
        q, k, v = proj[:, 0:512], proj[:, 512:1024], proj[:, 1024:1536]
        q_n = q * lax.rsqrt(_seg_sum(q * q, seg) * inv_hd + EPS) * qg_ref[g:g + 1, :]
        q_ref[:, g * 512:(g + 1) * 512] = (_rope(q_n, c64, s64, HEAD_DIM // 2) * ATTN_SCALE).astype(BF16)
        k_n = k * lax.rsqrt(_seg_sum(k * k, seg) * inv_hd + EPS) * kg_ref[g:g + 1, :]
        k_r = _rope(k_n, c64, s64, HEAD_DIM // 2)
        kv_ref[:, 0:512] = k_r
        kv_ref[:, 512:1024] = v
        kb_ref[:, 0:512] = k_r.astype(BF16)
        kb_ref[:, 512:1024] = v.astype(BF16)


def _odd_proj(x, tabs, wts, tm=256):
    t, d = x.shape
    tm = min(tm, t)
    c64, s64 = tabs
    ntab = c64.shape[0] // tm
    const = lambda a: pl.BlockSpec(a.shape, lambda i: (0,) * a.ndim)
    tab = lambda a: pl.BlockSpec((tm, a.shape[1]), lambda i: (i % ntab, 0))
    row = lambda n: pl.BlockSpec((tm, n), lambda i: (i, 0))
    consts = [wts[n] for n in ("g", "w_in", "seg", "qg", "kg")]
    return pl.pallas_call(
        _odd_proj_kernel,
        out_shape=(jax.ShapeDtypeStruct((t, 1536), BF16),) + (jax.ShapeDtypeStruct((t, 1024), F32),) * 3
        + (jax.ShapeDtypeStruct((t, 1024), BF16),) * 3,
        grid=(t // tm,),
        in_specs=[row(d)] + [const(a) for a in consts] + [tab(c64), tab(s64)],
        out_specs=(row(1536),) + (row(1024),) * 6,
        compiler_params=_cparams(("parallel",), 48),
        name="odd_proj",
    )(x, *consts, c64, s64)


def _dil_prompt_kernel(q_ref, kp_ref, kc_ref, vp_ref, vc_ref, o_ref, lse_ref):
    i = pl.program_id(2)
    t = q_ref.shape[1]
    row = lax.broadcasted_iota(jnp.int32, (t, 2 * t), 0)
    col = lax.broadcasted_iota(jnp.int32, (t, 2 * t), 1)
    allowed = (col >= row) & (col <= row + t) & ((col >= t) | (i > 0))
    q = q_ref[0]
    k = jnp.concatenate([kp_ref[0], kc_ref[0]], axis=0)
    v = jnp.concatenate([vp_ref[0], vc_ref[0]], axis=0)
    for h in range(DIL_HEADS):
        hs = slice(h * HEAD_DIM, (h + 1) * HEAD_DIM)
        s = jnp.where(allowed, _nt(q[:, hs], k[:, hs]), -jnp.inf)
        m = jnp.max(s, axis=-1, keepdims=True)
        p = jnp.exp(s - m)
        l = jnp.sum(p, axis=-1, keepdims=True)
        o_ref[0, :, hs] = _dot(p.astype(BF16), v[:, hs]) / l
        lse_ref[0, :, hs] = jnp.broadcast_to(m + jnp.log(l), (t, HEAD_DIM))


def _dil_prompt(q, kvb, g, dil, b, l):
    t = DIL_TILE
    ls = l // dil
    ni = ls // t
    assert l % dil == 0 and ls % t == 0
    qv = q.reshape(b, ls, dil * 1536)
    kv = kvb.reshape(b, ls, dil * 1024)
    blk = lambda f: pl.BlockSpec((1, t, 512), f)
    out = pl.pallas_call(
        _dil_prompt_kernel,
        out_shape=(jax.ShapeDtypeStruct((b, ls, dil * 512), F32),) * 2,
        grid=(b, dil, ni),
        in_specs=[
            blk(lambda bi, r, i: (bi, i, r * 3 + g)),
            blk(lambda bi, r, i: (bi, jnp.maximum(i - 1, 0), r * 2)),
            blk(lambda bi, r, i: (bi, i, r * 2)),
            blk(lambda bi, r, i: (bi, jnp.maximum(i - 1, 0), r * 2 + 1)),
            blk(lambda bi, r, i: (bi, i, r * 2 + 1)),
        ],
        out_specs=(blk(lambda bi, r, i: (bi, i, r)),) * 2,
        compiler_params=_cparams(("parallel", "parallel", "arbitrary"), 32),
        name=f"dil_prompt_{g}",
    )(qv, kv, kv, kv, kv)
    return out[0].reshape(b * l, 512), out[1].reshape(b * l, 512)


def _dil_sample_kernel(*refs, dil, n_tok, aliased):
    if aliased:
        qbd_ref, new_ref, st_ref, _, nst_ref, o_ref, lse_ref, newpad = refs
    else:
        qbd_ref, new_ref, st_ref, nst_ref, o_ref, lse_ref, newpad = refs
    feat, w = st_ref.shape
    half = feat // 2
    nq = qbd_ref.shape[1]
    heads = nq // n_tok
    newpad[...] = jnp.zeros_like(newpad)
    newpad[0:n_tok, :] = new_ref[0]
    npd = newpad[...]

    new_tail = pltpu.roll(npd.T, LANES - n_tok, 1)
    tail_lane = lax.broadcasted_iota(jnp.int32, (LANES, LANES), 1) >= LANES - n_tok
    for r0 in range(0, feat, LANES):
        shifted = pltpu.roll(st_ref[r0:r0 + LANES, :], w - n_tok, 1)
        nst_ref[r0:r0 + LANES, :] = shifted
        nst_ref[r0:r0 + LANES, w - LANES:w] = jnp.where(tail_lane, new_tail[r0:r0 + LANES, :], shifted[:, w - LANES:w])

    q = qbd_ref[0]
    kb = st_ref[0:half, :].astype(BF16)
    vb = st_ref[half:feat, :].astype(BF16)
    tok = lax.broadcasted_iota(jnp.int32, (nq, w), 0) // heads
    r = lax.broadcasted_iota(jnp.int32, (nq, w), 1)
    ok = (r >= tok) & (((w + tok - r) & (dil - 1)) == 0)
    s = jnp.where(ok, _dot(q, kb), -jnp.inf)

    tok2 = lax.broadcasted_iota(jnp.int32, (nq, LANES), 0) // heads
    j = lax.broadcasted_iota(jnp.int32, (nq, LANES), 1)
    ok2 = (j <= tok2) & (j < n_tok) & (((tok2 - j) & (dil - 1)) == 0)
    s2 = jnp.where(ok2, _nt(q, npd[:, 0:half].astype(BF16)), -jnp.inf)

    m = jnp.maximum(jnp.max(s, axis=-1, keepdims=True), jnp.max(s2, axis=-1, keepdims=True))
    p = jnp.exp(s - m)
    p2 = jnp.exp(s2 - m)
    l = jnp.sum(p, axis=-1, keepdims=True) + jnp.sum(p2, axis=-1, keepdims=True)
    o = (_nt(p.astype(BF16), vb) + _dot(p2.astype(BF16), npd[:, half:feat].astype(BF16))) / l
    head = lax.broadcasted_iota(jnp.int32, (nq, HEAD_DIM), 0) % heads
    o64 = jnp.zeros((nq, HEAD_DIM), F32)
    for h in range(DIL_HEADS):
        o64 = o64 + jnp.where(head == h, o[:, h * HEAD_DIM:(h + 1) * HEAD_DIM], 0.0)
    o_ref[0] = o64
    lse_ref[0] = jnp.broadcast_to(m + jnp.log(l), (nq, HEAD_DIM))


def _dil_sample(qbd, new, state, prev_out, li, dil):
    n_odd, b, feat, w = state.shape
    nq = qbd.shape[1]
    n_tok = new.shape[1]
    aliased = prev_out is not None
    in_specs = [
        pl.BlockSpec((1, nq, feat // 2), lambda i: (i, 0, 0)),
        pl.BlockSpec((1, n_tok, feat), lambda i: (i, 0, 0)),
        pl.BlockSpec((None, None, feat, w), lambda i: (li, i, 0, 0)),
    ]
    args = [qbd, new, state]
    if aliased:
        in_specs.append(pl.BlockSpec(memory_space=pl.ANY))
        args.append(prev_out)
    return pl.pallas_call(
        functools.partial(_dil_sample_kernel, dil=dil, n_tok=n_tok, aliased=aliased),
        out_shape=(
            jax.ShapeDtypeStruct(state.shape, F32),
            jax.ShapeDtypeStruct((b, nq, HEAD_DIM), F32),
            jax.ShapeDtypeStruct((b, nq, HEAD_DIM), F32),
        ),
        grid=(b,),
        in_specs=in_specs,
        out_specs=(
            pl.BlockSpec((None, None, feat, w), lambda i: (li, i, 0, 0)),
            pl.BlockSpec((1, nq, HEAD_DIM), lambda i: (i, 0, 0)),
            pl.BlockSpec((1, nq, HEAD_DIM), lambda i: (i, 0, 0)),
        ),
        scratch_shapes=[pltpu.VMEM((LANES, feat), F32)],
        input_output_aliases={3: 0} if aliased else {},
        compiler_params=_cparams(("parallel",), 56),
        name=f"dil_sample_{dil}",
    )(*args)


def _odd_out_kernel(x_ref, o0_ref, o1_ref, o2_ref, l0_ref, l1_ref, l2_ref, wo_ref, out_ref):
    l0, l1, l2 = l0_ref[...], l1_ref[...], l2_ref[...]
    m = jnp.maximum(jnp.maximum(l0, l1), l2)
    e0, e1, e2 = jnp.exp(l0 - m), jnp.exp(l1 - m), jnp.exp(l2 - m)
    o = (e0 * o0_ref[...] + e1 * o1_ref[...] + e2 * o2_ref[...]) / (e0 + e1 + e2)
    out_ref[...] = x_ref[...] + _dot(o.astype(BF16), wo_ref[...])


def _odd_out(x, outs, lses, w_out, tm=512):
    t, d = x.shape
    tm = min(tm, t)
    row = lambda n: pl.BlockSpec((tm, n), lambda i: (i, 0))
    return pl.pallas_call(
        _odd_out_kernel,
        out_shape=jax.ShapeDtypeStruct((t, d), F32),
        grid=(t // tm,),
        in_specs=[row(d)] + [row(512)] * 6 + [pl.BlockSpec(w_out.shape, lambda i: (0, 0))],
        out_specs=row(d),
        compiler_params=_cparams(("parallel",), 40),
        name="odd_out",
    )(x, *outs, *lses, w_out)


def _rope_tables(pos, dim, reps):
    half = dim // 2
    inv = ROPE_THETA ** (-jnp.arange(half, dtype=F32) / half)
    ang = pos.astype(F32)[:, None] * inv[None, :]
    c, s = jnp.cos(ang), jnp.sin(ang)
    return jnp.tile(jnp.concatenate([c, c], axis=-1), (1, reps)), jnp.tile(jnp.concatenate([-s, s], axis=-1), (1, reps))


def _seg_matrix(group_of_lane):
    g = np.asarray(group_of_lane)
    return jnp.asarray(g[:, None] == g[None, :], BF16)


def _block_diag_q(q, n_heads, lanes_of_head):
    b, t, _ = q.shape
    n_groups = max(lanes_of_head) + 1
    onehot = jnp.asarray(np.asarray(lanes_of_head)[:, None] == np.arange(n_groups)[None, :], q.dtype)
    qh = q.reshape(b, t, n_heads, 1, HEAD_DIM) * onehot[None, None, :, :, None]
    return qh.reshape(b, t * n_heads, n_groups * HEAD_DIM)


def kernel(x_prompt, x_sample, cache_moba_kv, cache_mla, state_dil128, state_dil512, state_dil2048, page_table,
           norm_g, ffn_w_in, ffn_w_out, even_w_in, moba_q_g, moba_k_g, mla_cq_g, mla_w_uq, mla_q_g, mla_ckv_g,
           mla_kr_g, mla_w_uk, mla_w_uv, even_w_out, odd_w_in, dil_q_g, dil_k_g, odd_w_out):
    bp, sp, d = x_prompt.shape
    bs, ts, _ = x_sample.shape
    depth = norm_g.shape[0]
    n_even, n_pool, page = cache_moba_kv.shape[:3]
    past = page_table.shape[1] * page
    pos_p = jnp.arange(sp, dtype=jnp.int32)
    pos_s = past + jnp.arange(ts, dtype=jnp.int32)
    tm_s = min(256, bs * ts)
    pos_s_tile = jnp.tile(pos_s, tm_s // ts)
    tab64_p, tab64_s = _rope_tables(pos_p, HEAD_DIM, 8), _rope_tables(pos_s_tile, HEAD_DIM, 8)
    tab32_p, tab32_s = _rope_tables(pos_p, MLA_ROPE, 8), _rope_tables(pos_s_tile, MLA_ROPE, 8)

    seg64 = _seg_matrix(np.arange(512) // HEAD_DIM)
    qdim = MLA_NOPE + MLA_ROPE
    perm = np.concatenate([np.concatenate([np.arange(h * qdim, h * qdim + MLA_NOPE) for h in range(MLA_HEADS)]),
                           np.concatenate([np.arange(h * qdim + MLA_NOPE, (h + 1) * qdim) for h in range(MLA_HEADS)])])
    m96 = _seg_matrix(perm // qdim)
    place = jnp.asarray((np.arange(256)[:, None] // MLA_ROPE == np.arange(1024)[None, :] // LANES)
                        & (np.arange(256)[:, None] % MLA_ROPE == np.arange(1024)[None, :] % LANES), BF16)
    eye_h = np.eye(MLA_HEADS, dtype=np.float32)

    cache_moba = jnp.transpose(cache_moba_kv, (0, 1, 3, 4, 5, 2)).reshape(
        n_even, n_pool, 2 * MOBA_KV_HEADS * HEAD_DIM, page)
    cache_mla_t = jnp.transpose(cache_mla, (0, 1, 3, 2))
    states = [jnp.transpose(s, (0, 1, 3, 4, 5, 2)).reshape(s.shape[0], s.shape[1], 2 * DIL_HEADS * HEAD_DIM, s.shape[2])
              for s in (state_dil128, state_dil512, state_dil2048)]
    new_states = [None, None, None]

    xp = x_prompt.reshape(bp * sp, d)
    xs = x_sample.reshape(bs * ts, d)
    moba_p, moba_s, mla_p, mla_s = [], [], [], []
    dil_p = [[] for _ in DIL_CONFIGS]

    for layer in range(depth):
        li = layer // 2
        w_in = ffn_w_in[layer, 0].astype(BF16)
        w_out = ffn_w_out[layer, 0].astype(BF16)
        xp = _ffn(xp, norm_g[layer, 0], w_in, w_out)
        xs = _ffn(xs, norm_g[layer, 0], w_in, w_out)
        if layer % 2 == 0:
            wts = {
                "g": norm_g[layer, 1].reshape(1, d),
                "w_in": jnp.pad(even_w_in[li], ((0, 0), (0, 1664 - even_w_in.shape[2]))).astype(BF16),
                "seg": seg64,
                "m96": m96,
                "wuq": mla_w_uq[li][:, perm].astype(BF16),
                "wuk": (jnp.transpose(mla_w_uk[li], (1, 2, 0))[:, :, None, :] * eye_h[:, None, :, None]
                        ).reshape(MLA_HEADS * MLA_NOPE, MLA_HEADS * MLA_KV_LORA).astype(BF16),
                "place": place,
                "qg": jnp.tile(moba_q_g[li], MOBA_HEADS).reshape(1, 512),
                "kg": jnp.tile(moba_k_g[li], MOBA_KV_HEADS).reshape(1, 256),
                "cqg": mla_cq_g[li].reshape(1, MLA_Q_LORA),
                "mqg": jnp.tile(mla_q_g[li], MLA_HEADS)[perm].reshape(1, MLA_HEADS * qdim),
                "ckvg": mla_ckv_g[li].reshape(1, MLA_KV_LORA),
                "krg": jnp.pad(mla_kr_g[li], (0, LANES - MLA_ROPE)).reshape(1, LANES),
            }
            wuv_bd = (jnp.transpose(mla_w_uv[li], (1, 0, 2))[:, :, None, :] * eye_h[:, None, :, None]
                      ).reshape(MLA_HEADS * MLA_KV_LORA, MLA_HEADS * MLA_V).astype(BF16)
            w_o = even_w_out[li].astype(BF16)
            qa, kv, lat, qf, latk = _even_proj(xp, tab64_p + tab32_p, wts)
            oa = _moba_prompt(qa, kv, bp, sp)
            ol = _mla_prompt(qf, latk, bp, sp)
            xp = _even_out(xp, oa, ol, wuv_bd, w_o)
            moba_p.append(kv.reshape(bp, sp, 2, MOBA_KV_HEADS, HEAD_DIM))
            mla_p.append(lat.reshape(bp, sp, MLA_KV_LORA + MLA_ROPE))
            qa, kv, lat, qf, latk = _even_proj(xs, tab64_s + tab32_s, wts)
            kvh_of_head = [h // (MOBA_HEADS // MOBA_KV_HEADS) for h in range(MOBA_HEADS)]
            qbd = _block_diag_q(qa.reshape(bs, ts, 512), MOBA_HEADS, kvh_of_head)
            oa, ol = _even_sample(page_table, qbd, kv.reshape(bs, ts, 512), qf.reshape(bs, ts * MLA_HEADS, MLA_QW),
                                  latk.astype(F32).reshape(bs, ts, MLA_QW), cache_moba, cache_mla_t, li)
            xs = _even_out(xs, oa.reshape(bs * ts, 512), ol.reshape(bs * ts, MLA_HEADS * MLA_KV_LORA), wuv_bd, w_o)
            moba_s.append(kv.reshape(bs, ts, 2, MOBA_KV_HEADS, HEAD_DIM))
            mla_s.append(lat.reshape(bs, ts, MLA_KV_LORA + MLA_ROPE))
        else:
            wts = {
                "g": norm_g[layer, 1].reshape(1, d),
                "w_in": odd_w_in[li].astype(BF16),
                "seg": seg64,
                "qg": jnp.tile(dil_q_g[li], (1, DIL_HEADS)),
                "kg": jnp.tile(dil_k_g[li], (1, DIL_HEADS)),
            }
            w_o = odd_w_out[li].astype(BF16)
            q, kv0, kv1, kv2, kb0, kb1, kb2 = _odd_proj(xp, tab64_p, wts)
            outs, lses = [], []
            for g, ((w, dil), kvf, kvb) in enumerate(zip(DIL_CONFIGS, (kv0, kv1, kv2), (kb0, kb1, kb2))):
                o, lse = _dil_prompt(q, kvb, g, dil, bp, sp)
                outs.append(o)
                lses.append(lse)
                wp = min(w, sp)
                dil_p[g].append(kvf.reshape(bp, sp, 2, DIL_HEADS, HEAD_DIM)[:, sp - wp:])
            xp = _odd_out(xp, outs, lses, w_o)
            q, kv0, kv1, kv2, _, _, _ = _odd_proj(xs, tab64_s, wts)
            outs, lses = [], []
            for g, ((w, dil), kvf) in enumerate(zip(DIL_CONFIGS, (kv0, kv1, kv2))):
                qbd = _block_diag_q(q[:, g * 512:(g + 1) * 512].reshape(bs, ts, 512), DIL_HEADS, list(range(DIL_HEADS)))
                nst, o, lse = _dil_sample(qbd, kvf.reshape(bs, ts, 1024), states[g], new_states[g], li, dil)
                new_states[g] = nst
                outs.append(o.reshape(bs * ts, 512))
                lses.append(lse.reshape(bs * ts, 512))
            xs = _odd_out(xs, outs, lses, w_o)
        w_in = ffn_w_in[layer, 1].astype(BF16)
        w_out = ffn_w_out[layer, 1].astype(BF16)
        xp = _ffn(xp, norm_g[layer, 2], w_in, w_out)
        xs = _ffn(xs, norm_g[layer, 2], w_in, w_out)

    dil_s = [jnp.transpose(ns.reshape(ns.shape[0], ns.shape[1], 2, DIL_HEADS, HEAD_DIM, ns.shape[3]), (0, 1, 5, 2, 3, 4))
             for ns in new_states]
    return (xp.reshape(bp, sp, d), xs.reshape(bs, ts, d),
            jnp.stack(moba_p, axis=0), jnp.stack(moba_s, axis=0),
            jnp.stack(mla_p, axis=0), jnp.stack(mla_s, axis=0),
            jnp.stack(dil_p[0], axis=0), dil_s[0],
            jnp.stack(dil_p[1], axis=0), dil_s[1],
            jnp.stack(dil_p[2], axis=0), dil_s[2])
```

```python
import functools

import numpy as np
import jax
import jax.numpy as jnp
from jax import lax
from jax.experimental import pallas as pl
from jax.experimental.pallas import tpu as pltpu

F32 = jnp.float32
BF16 = jnp.bfloat16

HEAD_DIM = 64
ROPE_THETA = 10000.0
EPS = 1e-6
ATTN_SCALE = HEAD_DIM ** -0.5
MOBA_HEADS = 8
MOBA_KV_HEADS = 4
MOBA_BLOCK = 256
MOBA_TOPK = 3
MLA_HEADS = 8
MLA_Q_LORA = 256
MLA_KV_LORA = 256
MLA_NOPE = 64
MLA_ROPE = 32
MLA_V = 64
MLA_SCALE = (MLA_NOPE + MLA_ROPE) ** -0.5
MLA_QW = 384
DIL_CONFIGS = ((128, 1), (512, 4), (2048, 16))
DIL_HEADS = 8
DIL_TILE = 128
LANES = 128
PAGES_PER_CHUNK = 16
PAGE_BUFFERS = 3
V7X_VMEM_BYTES = 64 * 1024 * 1024


def _cparams(sem, vmem_mb):
    assert vmem_mb * 1024 * 1024 < V7X_VMEM_BYTES
    return pltpu.CompilerParams(dimension_semantics=sem, vmem_limit_bytes=vmem_mb * 1024 * 1024)


def _nt(a, b):
    return lax.dot_general(a, b, (((1,), (1,)), ((), ())), preferred_element_type=F32)


def _dot(a, b):
    return jnp.dot(a, b, preferred_element_type=F32)


def _split(a):
    hi = a.astype(BF16)
    lo = (a - hi.astype(F32)).astype(BF16)
    return hi, lo


def _seg_sum(a, m):
    hi, lo = _split(a)
    return _dot(hi, m) + _dot(lo, m)


def _nt_hp(a, b):
    ah, al = _split(a)
    bh, bl = _split(b)
    return _nt(ah, bh) + _nt(ah, bl) + _nt(al, bh)


def _lanes(x, n):
    return x if n == LANES else jnp.concatenate([x] * (n // LANES), axis=1)


def _dot_hp(a, b):
    ah, al = _split(a)
    bh, bl = _split(b)
    return _dot(ah, bh) + _dot(ah, bl) + _dot(al, bh)


def _rms(x, g):
    return x * lax.rsqrt(jnp.mean(x * x, axis=-1, keepdims=True) + EPS) * g


def _rope(x, cos, sin, half):
    c = x.shape[-1]
    lane = lax.broadcasted_iota(jnp.int32, x.shape, 1)
    first = (lane & (2 * half - 1)) < half
    partner = jnp.where(first, pltpu.roll(x, c - half, 1), pltpu.roll(x, half, 1))
    return x * cos + partner * sin


def _ffn_kernel(x_ref, g_ref, wg_ref, wu_ref, wo_ref, o_ref, xn_ref, acc_ref):
    j = pl.program_id(1)

    @pl.when(j == 0)
    def _():
        xn_ref[...] = _rms(x_ref[...], g_ref[...]).astype(BF16)
        acc_ref[...] = jnp.zeros_like(acc_ref)

    xn = xn_ref[...]
    gate = _dot(xn, wg_ref[...])
    up = _dot(xn, wu_ref[...])
    h = (gate * jax.nn.sigmoid(gate) * up).astype(BF16)
    acc_ref[...] += _dot(h, wo_ref[...])

    @pl.when(j == pl.num_programs(1) - 1)
    def _():
        o_ref[...] = x_ref[...] + 0.5 * acc_ref[...]


def _ffn(x, g, w_in, w_out, tm=512, tf=1408):
    t, d = x.shape
    tm = min(tm, t)
    dff = w_out.shape[0]
    nf = dff // tf
    assert t % tm == 0 and dff % tf == 0
    return pl.pallas_call(
        _ffn_kernel,
        out_shape=jax.ShapeDtypeStruct((t, d), F32),
        grid=(t // tm, nf),
        in_specs=[
            pl.BlockSpec((tm, d), lambda i, j: (i, 0)),
            pl.BlockSpec((1, d), lambda i, j: (0, 0)),
            pl.BlockSpec((d, tf), lambda i, j: (0, j)),
            pl.BlockSpec((d, tf), lambda i, j: (0, j + nf)),
            pl.BlockSpec((tf, d), lambda i, j: (j, 0)),
        ],
        out_specs=pl.BlockSpec((tm, d), lambda i, j: (i, 0)),
        scratch_shapes=[pltpu.VMEM((tm, d), BF16), pltpu.VMEM((tm, d), F32)],
        compiler_params=_cparams(("parallel", "arbitrary"), 48),
        name="ffn",
    )(x, g.reshape(1, d), w_in, w_in, w_out)


def _even_proj_kernel(x_ref, g_ref, w_ref, seg_ref, m96_ref, wuq_ref, wuk_ref, place_ref,
                      qg_ref, kg_ref, cqg_ref, mqg_ref, ckvg_ref, krg_ref,
                      c64_ref, s64_ref, c32_ref, s32_ref,
                      qa_ref, kv_ref, lat_ref, qf_ref, latk_ref):
    xn = _rms(x_ref[...], g_ref[...]).astype(BF16)
    proj = _dot(xn, w_ref[...])
    qa, ka, va = proj[:, 0:512], proj[:, 512:768], proj[:, 768:1024]
    cq, ckv, kr = proj[:, 1024:1280], proj[:, 1280:1536], proj[:, 1536:1664]
    seg = seg_ref[...]
    c64, s64, c32, s32 = c64_ref[...], s64_ref[...], c32_ref[...], s32_ref[...]
    inv_hd = 1.0 / HEAD_DIM

    qa_n = qa * lax.rsqrt(_seg_sum(qa * qa, seg) * inv_hd + EPS) * qg_ref[...]
    qa_ref[...] = _rope(qa_n, c64, s64, HEAD_DIM // 2)
    ka_n = ka * lax.rsqrt(_seg_sum(ka * ka, seg[:256, :256]) * inv_hd + EPS) * kg_ref[...]
    kv_ref[:, 0:256] = _rope(ka_n, c64[:, :256], s64[:, :256], HEAD_DIM // 2)
    kv_ref[:, 256:512] = va

    cq_n = _rms(cq, cqg_ref[...]).astype(BF16)
    q = _dot(cq_n, wuq_ref[...])
    q_n = q * lax.rsqrt(_seg_sum(q * q, m96_ref[...]) * (1.0 / (MLA_NOPE + MLA_ROPE)) + EPS) * mqg_ref[...]
    q_lat = _dot(q_n[:, :512].astype(BF16), wuk_ref[...])
    q_rope = _rope(q_n[:, 512:], c32, s32, MLA_ROPE // 2).astype(BF16)
    tails = _dot(q_rope, place_ref[...])
    for h in range(MLA_HEADS):
        qf_ref[:, h * MLA_QW:h * MLA_QW + 256] = q_lat[:, h * 256:(h + 1) * 256].astype(BF16)
        qf_ref[:, h * MLA_QW + 256:(h + 1) * MLA_QW] = tails[:, h * LANES:(h + 1) * LANES].astype(BF16)

    ckv_n = _rms(ckv, ckvg_ref[...])
    kr_n = kr * lax.rsqrt(jnp.sum(kr * kr, axis=-1, keepdims=True) * (1.0 / MLA_ROPE) + EPS) * krg_ref[...]
    kr_r = _rope(kr_n, c32[:, :LANES], s32[:, :LANES], MLA_ROPE // 2)
    lat_ref[:, 0:256] = ckv_n
    lat_ref[:, 256:288] = kr_r[:, :MLA_ROPE]
    latk_ref[:, 0:256] = ckv_n.astype(BF16)
    latk_ref[:, 256:384] = kr_r.astype(BF16)


def _even_proj(x, tabs, wts, tm=256):
    t, d = x.shape
    tm = min(tm, t)
    c64, s64, c32, s32 = tabs
    ntab = c64.shape[0] // tm
    const = lambda a: pl.BlockSpec(a.shape, lambda i: (0,) * a.ndim)
    tab = lambda a: pl.BlockSpec((tm, a.shape[1]), lambda i: (i % ntab, 0))
    row = lambda n: pl.BlockSpec((tm, n), lambda i: (i, 0))
    names = ("g", "w_in", "seg", "m96", "wuq", "wuk", "place", "qg", "kg", "cqg", "mqg", "ckvg", "krg")
    consts = [wts[n] for n in names]
    return pl.pallas_call(
        _even_proj_kernel,
        out_shape=(
            jax.ShapeDtypeStruct((t, 512), F32),
            jax.ShapeDtypeStruct((t, 512), F32),
            jax.ShapeDtypeStruct((t, 288), F32),
            jax.ShapeDtypeStruct((t, MLA_HEADS * MLA_QW), BF16),
            jax.ShapeDtypeStruct((t, MLA_QW), BF16),
        ),
        grid=(t // tm,),
        in_specs=[row(d)] + [const(a) for a in consts] + [tab(c64), tab(s64), tab(c32), tab(s32)],
        out_specs=(row(512), row(512), row(288), row(MLA_HEADS * MLA_QW), row(MLA_QW)),
        compiler_params=_cparams(("parallel",), 48),
        name="even_proj",
    )(x, *consts, c64, s64, c32, s32)


def _moba_prompt_kernel(q_ref, kv_ref, o_ref, kb_ref, vb_ref, km_ref, *, nb):
    own = pl.program_id(1)
    tq = q_ref.shape[0]
    grp = MOBA_HEADS // MOBA_KV_HEADS
    rows = grp * tq

    @pl.when(own == 0)
    def _():
        for k in range(MOBA_KV_HEADS):
            kb_ref[k] = kv_ref[:, k * HEAD_DIM:(k + 1) * HEAD_DIM].astype(BF16)
            vb_ref[k] = kv_ref[:, 256 + k * HEAD_DIM:256 + (k + 1) * HEAD_DIM].astype(BF16)
        km_ref[...] = jnp.zeros_like(km_ref)
        for n in range(nb):
            km_ref[n:n + 1, :] = jnp.mean(kv_ref[n * MOBA_BLOCK:(n + 1) * MOBA_BLOCK, 0:256], axis=0, keepdims=True)

    lane = lax.broadcasted_iota(jnp.int32, (rows, LANES), 1)
    past = lane < own
    causal = (lax.broadcasted_iota(jnp.int32, (rows, MOBA_BLOCK), 1)
              <= (lax.broadcasted_iota(jnp.int32, (rows, MOBA_BLOCK), 0) & (tq - 1)))
    own_rows = pl.ds(pl.multiple_of(own * MOBA_BLOCK, MOBA_BLOCK), MOBA_BLOCK)
    inits, per_k = [], []
    for k in range(MOBA_KV_HEADS):
        q2 = jnp.concatenate([q_ref[:, (k * grp + g) * HEAD_DIM:(k * grp + g + 1) * HEAD_DIM] for g in range(grp)],
                             axis=0)
        gate = jnp.where(past, _nt_hp(q2, km_ref[:, k * HEAD_DIM:(k + 1) * HEAD_DIM]), -jnp.inf)
        beaten = jnp.zeros((rows, LANES), F32)
        for m in range(nb):
            gm = gate[:, m:m + 1]
            ahead = (gm > gate) | ((gm == gate) & (lane > m))
            beaten = beaten + jnp.where(ahead, 1.0, 0.0)
        sel = jnp.where(past & (beaten < MOBA_TOPK), 1.0, 0.0)
        qs = (q2 * ATTN_SCALE).astype(BF16)
        s = jnp.where(causal, _nt(qs, kb_ref[k, own_rows, :]), -jnp.inf)
        m0 = jnp.broadcast_to(jnp.max(s, axis=-1, keepdims=True), (rows, LANES))
        p = jnp.exp(s - _lanes(m0, MOBA_BLOCK))
        inits.append((m0, jnp.broadcast_to(jnp.sum(p, axis=-1, keepdims=True), (rows, LANES)),
                      _dot(p.astype(BF16), vb_ref[k, own_rows, :])))
        per_k.append((qs, sel))

    def body(n, carry):
        blk = pl.ds(pl.multiple_of(n * MOBA_BLOCK, MOBA_BLOCK), MOBA_BLOCK)
        pick = lane == n
        out = []
        for k, ((m, l, acc), (qs, sel)) in enumerate(zip(carry, per_k)):
            chosen = jnp.sum(jnp.where(pick, sel, 0.0), axis=-1, keepdims=True) > 0.5
            s = jnp.where(chosen, _nt(qs, kb_ref[k, blk, :]), -jnp.inf)
            m_new = jnp.maximum(m, jnp.max(s, axis=-1, keepdims=True))
            alpha = jnp.exp(m - m_new)
            p = jnp.exp(s - _lanes(m_new, MOBA_BLOCK))
            out.append((m_new, alpha * l + jnp.sum(p, axis=-1, keepdims=True),
                        alpha[:, :HEAD_DIM] * acc + _dot(p.astype(BF16), vb_ref[k, blk, :])))
        return tuple(out)

    for k, (_, l, acc) in enumerate(lax.fori_loop(0, own, body, tuple(inits))):
        o = acc / l[:, :HEAD_DIM]
        for g in range(grp):
            h = k * grp + g
            o_ref[:, h * HEAD_DIM:(h + 1) * HEAD_DIM] = o[g * tq:(g + 1) * tq, :].astype(o_ref.dtype)


def _moba_prompt(qa, kv, b, l):
    nb = l // MOBA_BLOCK
    assert l % MOBA_BLOCK == 0 and nb <= LANES
    return pl.pallas_call(
        functools.partial(_moba_prompt_kernel, nb=nb),
        out_shape=jax.ShapeDtypeStruct((b * l, 512), BF16),
        grid=(b, nb),
        in_specs=[
            pl.BlockSpec((MOBA_BLOCK, 512), lambda i, j: (i * nb + j, 0)),
            pl.BlockSpec((l, 512), lambda i, j: (i, 0)),
        ],
        out_specs=pl.BlockSpec((MOBA_BLOCK, 512), lambda i, j: (i * nb + j, 0)),
        scratch_shapes=[pltpu.VMEM((MOBA_KV_HEADS, l, HEAD_DIM), BF16), pltpu.VMEM((MOBA_KV_HEADS, l, HEAD_DIM), BF16),
                        pltpu.VMEM((LANES, 256), F32)],
        compiler_params=_cparams(("parallel", "arbitrary"), 48),
        name="moba_prompt",
    )(qa, kv)


def _mla_prompt_kernel(q_ref, k_ref, o_ref, m_ref, l_ref, acc_ref, s_ref, p_ref, *, tq, tk):
    qi = pl.program_id(1)
    rows = MLA_HEADS * tq
    q = jnp.concatenate([q_ref[:, h * MLA_QW:(h + 1) * MLA_QW] for h in range(MLA_HEADS)], axis=0)
    qpos = lax.broadcasted_iota(jnp.int32, (tq, tk), 0) + qi * tq
    col = lax.broadcasted_iota(jnp.int32, (tq, tk), 1)
    m_ref[...] = jnp.full_like(m_ref, -jnp.inf)
    l_ref[...] = jnp.zeros_like(l_ref)
    acc_ref[...] = jnp.zeros_like(acc_ref)

    def body(c, carry):
        k = k_ref[pl.ds(pl.multiple_of(c * tk, tk), tk), :]
        s_ref[...] = _nt(q, k)
        for h in range(MLA_HEADS):
            rs = slice(h * tq, (h + 1) * tq)
            s = jnp.where(col + c * tk <= qpos, s_ref[rs, :] * MLA_SCALE, -jnp.inf)
            m_old = m_ref[rs, :]
            m_new = jnp.maximum(m_old, jnp.max(s, axis=-1, keepdims=True))
            alpha = jnp.exp(m_old - m_new)
            p = jnp.exp(s - _lanes(m_new, tk))
            l_ref[rs, :] = alpha * l_ref[rs, :] + jnp.sum(p, axis=-1, keepdims=True)
            acc_ref[rs, :] = _lanes(alpha, MLA_KV_LORA) * acc_ref[rs, :]
            p_ref[rs, :] = p.astype(BF16)
            m_ref[rs, :] = m_new
        acc_ref[...] += _dot(p_ref[...], k[:, :MLA_KV_LORA])
        return carry

    lax.fori_loop(0, (qi * tq + tq + tk - 1) // tk, body, 0)
    for h in range(MLA_HEADS):
        rs = slice(h * tq, (h + 1) * tq)
        o_ref[:, h * MLA_KV_LORA:(h + 1) * MLA_KV_LORA] = (acc_ref[rs, :] / _lanes(l_ref[rs, :], MLA_KV_LORA)).astype(o_ref.dtype)


def _mla_prompt(qf, latk, b, l, tq=128, tk=256):
    nq = l // tq
    assert l % tq == 0 and l % tk == 0 and tq & (tq - 1) == 0
    rows = MLA_HEADS * tq
    return pl.pallas_call(
        functools.partial(_mla_prompt_kernel, tq=tq, tk=tk),
        out_shape=jax.ShapeDtypeStruct((b * l, MLA_HEADS * MLA_KV_LORA), BF16),
        grid=(b, nq),
        in_specs=[
            pl.BlockSpec((tq, MLA_HEADS * MLA_QW), lambda i, j: (i * nq + j, 0)),
            pl.BlockSpec((l, MLA_QW), lambda i, j: (i, 0)),
        ],
        out_specs=pl.BlockSpec((tq, MLA_HEADS * MLA_KV_LORA), lambda i, j: (i * nq + j, 0)),
        scratch_shapes=[pltpu.VMEM((rows, LANES), F32), pltpu.VMEM((rows, LANES), F32),
                        pltpu.VMEM((rows, MLA_KV_LORA), F32),
                        pltpu.VMEM((rows, tk), F32), pltpu.VMEM((rows, tk), BF16)],
        compiler_params=_cparams(("parallel", "arbitrary"), 32),
        name="mla_prompt",
    )(qf, latk)


def _even_sample_kernel(pt_ref, qbd_ref, kvnew_ref, qf_ref, latnew_ref, cm_hbm, cl_hbm,
                        oa_ref, ol_ref,
                        mbuf, lbuf, sem, km_ref, mpart, lpart, opart, newkv, newlat, m_ref, l_ref, acc_ref,
                        *, li, nch, n_tok):
    step = pl.program_id(0)
    nsteps = pl.num_programs(0)
    c = step % nch
    nbuf = mbuf.shape[0]
    slot = step % nbuf
    ppc = mbuf.shape[1]
    page = mbuf.shape[3]
    nkeys = ppc * page
    bpc = nkeys // MOBA_BLOCK
    nblk = nch * bpc
    nq = qbd_ref.shape[1]
    heads = nq // n_tok

    def copies(s, sl):
        base = s * ppc
        out = []
        for p in range(ppc):
            pg = pt_ref[base + p]
            out.append(pltpu.make_async_copy(cm_hbm.at[li, pg], mbuf.at[sl, p], sem.at[0, sl]))
            out.append(pltpu.make_async_copy(cl_hbm.at[li, pg], lbuf.at[sl, p], sem.at[1, sl]))
        return out

    @pl.when(step == 0)
    def _():
        for ahead in range(nbuf - 1):
            @pl.when(ahead < nsteps)
            def _(ahead=ahead):
                for cp in copies(ahead, ahead):
                    cp.start()

    @pl.when(step + nbuf - 1 < nsteps)
    def _():
        for cp in copies(step + nbuf - 1, (step + nbuf - 1) % nbuf):
            cp.start()

    for cp in copies(step, slot):
        cp.wait()

    lane = lax.broadcasted_iota(jnp.int32, (nq, LANES), 1)
    tok = lax.broadcasted_iota(jnp.int32, (nq, LANES), 0) // heads

    @pl.when(c == 0)
    def _():
        m_ref[...] = jnp.full_like(m_ref, -jnp.inf)
        l_ref[...] = jnp.zeros_like(l_ref)
        acc_ref[...] = jnp.zeros_like(acc_ref)
        mpart[...] = jnp.zeros_like(mpart)
        lpart[...] = jnp.zeros_like(lpart)
        km_ref[...] = jnp.zeros_like(km_ref)

    qbd = qbd_ref[0]
    qs = (qbd * ATTN_SCALE).astype(BF16)
    ppb = MOBA_BLOCK // page
    klane = lax.broadcasted_iota(jnp.int32, (256, LANES), 1)
    scores = [_dot(qs, mbuf[slot, p, 0:256, :].astype(BF16)) for p in range(ppc)]
    mp = mpart[...]
    lp = lpart[...]
    km = km_ref[...]
    for n in range(bpc):
        blk = c * bpc + n
        pages = range(n * ppb, (n + 1) * ppb)
        s_n = jnp.concatenate([scores[p] for p in pages], axis=1)
        m_n = jnp.max(s_n, axis=-1, keepdims=True)
        p_n = jnp.exp(s_n - m_n)
        mp = jnp.where(lane == blk, m_n, mp)
        lp = jnp.where(lane == blk, jnp.sum(p_n, axis=-1, keepdims=True), lp)
        pb = p_n.astype(BF16)
        opart[blk] = sum(_nt(pb[:, i * page:(i + 1) * page], mbuf[slot, p, 256:512, :].astype(BF16))
                         for i, p in enumerate(pages))
        ksum = sum(mbuf[slot, p, 0:256, :] for p in pages)
        km = jnp.where(klane == blk, jnp.sum(ksum, axis=1, keepdims=True) * (1.0 / MOBA_BLOCK), km)
    mpart[...] = mp
    lpart[...] = lp
    km_ref[...] = km

    qf = qf_ref[0]
    lat = [lbuf[slot, p].astype(BF16) for p in range(ppc)]
    s = jnp.concatenate([_dot(qf[:, 0:256], lt[0:256, :]) + _dot(qf[:, 256:288], lt[256:288, :]) for lt in lat],
                        axis=1) * MLA_SCALE
    m_old = m_ref[...]
    m_new = jnp.maximum(m_old, jnp.max(s, axis=-1, keepdims=True))
    alpha = jnp.exp(m_old - m_new)
    p = jnp.exp(s - m_new).astype(BF16)
    l_ref[...] = alpha * l_ref[...] + jnp.sum(p.astype(F32), axis=-1, keepdims=True)
    acc = alpha * acc_ref[...]
    for pg in range(ppc):
        acc = acc + _nt(p[:, pg * page:(pg + 1) * page], lat[pg][0:256, :])
    acc_ref[...] = acc
    m_ref[...] = m_new

    @pl.when(c == nch - 1)
    def _():
        new_ok = (lane <= tok) & (lane < n_tok)
        gate = jnp.where(lane < nblk, _dot_hp(qbd, km_ref[...]), -jnp.inf)
        beaten = jnp.zeros((nq, LANES), F32)
        for m in range(nblk):
            gm = gate[:, m:m + 1]
            ahead = (gm > gate) | ((gm == gate) & (lane > m))
            beaten = beaten + jnp.where(ahead, 1.0, 0.0)
        sel = (lane < nblk) & (beaten < MOBA_TOPK)
        newkv[...] = jnp.zeros_like(newkv)
        newkv[0:n_tok, :] = kvnew_ref[0]
        nk = newkv[...]
        s_new = jnp.where(new_ok, _nt(qs, nk[:, 0:256].astype(BF16)), -jnp.inf)
        mp2 = mpart[...]
        m_all = jnp.maximum(jnp.max(jnp.where(sel, mp2, -jnp.inf), axis=-1, keepdims=True),
                            jnp.max(s_new, axis=-1, keepdims=True))
        w = jnp.where(sel, jnp.exp(mp2 - m_all), 0.0)
        p_new = jnp.exp(s_new - m_all)
        l_tot = jnp.sum(w * lpart[...], axis=-1, keepdims=True) + jnp.sum(p_new, axis=-1, keepdims=True)
        o = _dot(p_new.astype(BF16), nk[:, 256:512].astype(BF16))
        for n in range(nblk):
            o = o + w[:, n:n + 1] * opart[n]
        o = o / l_tot
        rowi = lax.broadcasted_iota(jnp.int32, (nq, HEAD_DIM), 0)
        kvh = (rowi % heads) // (MOBA_HEADS // MOBA_KV_HEADS)
        o64 = jnp.zeros((nq, HEAD_DIM), F32)
        for k in range(MOBA_KV_HEADS):
            o64 = o64 + jnp.where(kvh == k, o[:, k * HEAD_DIM:(k + 1) * HEAD_DIM], 0.0)
        oa_ref[0] = o64.astype(oa_ref.dtype)

        newlat[...] = jnp.zeros_like(newlat)
        newlat[0:n_tok, :] = latnew_ref[0]
        nl = newlat[...].astype(BF16)
        s2 = jnp.where(new_ok, _nt(qf, nl) * MLA_SCALE, -jnp.inf)
        m_o = m_ref[...]
        m_f = jnp.maximum(m_o, jnp.max(s2, axis=-1, keepdims=True))
        a_f = jnp.exp(m_o - m_f)
        p2 = jnp.exp(s2 - m_f)
        l_f = a_f * l_ref[...] + jnp.sum(p2, axis=-1, keepdims=True)
        acc = a_f * acc_ref[...] + _dot(p2.astype(BF16), nl[:, 0:256])
        ol_ref[0] = (acc / l_f).astype(ol_ref.dtype)


def _even_sample(page_table, qbd, kvnew, qf, latnew, cache_moba, cache_mla, li):
    b, nq, _ = qbd.shape
    n_tok = kvnew.shape[1]
    n_pages = page_table.shape[1]
    page = cache_moba.shape[3]
    ppc = PAGES_PER_CHUNK
    nch = n_pages // ppc
    assert n_pages % ppc == 0 and (ppc * page) % MOBA_BLOCK == 0 and n_pages * page // MOBA_BLOCK <= LANES
    nblk = n_pages * page // MOBA_BLOCK
    grid_spec = pltpu.PrefetchScalarGridSpec(
        num_scalar_prefetch=1,
        grid=(b * nch,),
        in_specs=[
            pl.BlockSpec((1, nq, 256), lambda s, pt: (s // nch, 0, 0)),
            pl.BlockSpec((1, n_tok, 512), lambda s, pt: (s // nch, 0, 0)),
            pl.BlockSpec((1, nq, MLA_QW), lambda s, pt: (s // nch, 0, 0)),
            pl.BlockSpec((1, n_tok, MLA_QW), lambda s, pt: (s // nch, 0, 0)),
            pl.BlockSpec(memory_space=pl.ANY),
            pl.BlockSpec(memory_space=pl.ANY),
        ],
        out_specs=(
            pl.BlockSpec((1, nq, HEAD_DIM), lambda s, pt: (s // nch, 0, 0)),
            pl.BlockSpec((1, nq, MLA_KV_LORA), lambda s, pt: (s // nch, 0, 0)),
        ),
        scratch_shapes=[
            pltpu.VMEM((PAGE_BUFFERS, ppc, 512, page), F32),
            pltpu.VMEM((PAGE_BUFFERS, ppc, 288, page), F32),
            pltpu.SemaphoreType.DMA((2, PAGE_BUFFERS)),
            pltpu.VMEM((256, LANES), F32),
            pltpu.VMEM((nq, LANES), F32),
            pltpu.VMEM((nq, LANES), F32),
            pltpu.VMEM((nblk, nq, 256), F32),
            pltpu.VMEM((LANES, 512), F32),
            pltpu.VMEM((LANES, MLA_QW), F32),
            pltpu.VMEM((nq, 1), F32),
            pltpu.VMEM((nq, 1), F32),
            pltpu.VMEM((nq, MLA_KV_LORA), F32),
        ],
    )
    return pl.pallas_call(
        functools.partial(_even_sample_kernel, li=li, nch=nch, n_tok=n_tok),
        out_shape=(jax.ShapeDtypeStruct((b, nq, HEAD_DIM), BF16), jax.ShapeDtypeStruct((b, nq, MLA_KV_LORA), BF16)),
        grid_spec=grid_spec,
        compiler_params=_cparams(("arbitrary",), 48),
        name="even_sample",
    )(page_table.reshape(-1), qbd, kvnew, qf, latnew, cache_moba, cache_mla)


def _even_out_kernel(x_ref, oa_ref, ol_ref, wuv_ref, wo_ref, o_ref):
    ob = _dot(ol_ref[...], wuv_ref[...]).astype(BF16)
    y = _dot(oa_ref[...], wo_ref[0:512, :]) + _dot(ob, wo_ref[512:1024, :])
    o_ref[...] = x_ref[...] + y


def _even_out(x, oa, ol, wuv_bd, w_out, tm=512):
    t, d = x.shape
    tm = min(tm, t)
    row = lambda n: pl.BlockSpec((tm, n), lambda i: (i, 0))
    const = lambda a: pl.BlockSpec(a.shape, lambda i: (0,) * a.ndim)
    return pl.pallas_call(
        _even_out_kernel,
        out_shape=jax.ShapeDtypeStruct((t, d), F32),
        grid=(t // tm,),
        in_specs=[row(d), row(512), row(MLA_HEADS * MLA_KV_LORA), const(wuv_bd), const(w_out)],
        out_specs=row(d),
        compiler_params=_cparams(("parallel",), 40),
        name="even_out",
    )(x, oa, ol, wuv_bd, w_out)


def _odd_proj_kernel(*refs, dils):
    x_ref, g_ref, w_ref, seg_ref, qg_ref, kg_ref, c64_ref, s64_ref = refs[:8]
    if dils is None:
        q_ref, kv_refs = refs[8], refs[9:12]
    else:
        q_refs, kv_refs, kb_refs, scr = refs[8:11], refs[11:14], refs[14:17], refs[17:]
    tm = x_ref.shape[0]
    xn = _rms(x_ref[...], g_ref[...]).astype(BF16)
    seg = seg_ref[...]
    c64, s64 = c64_ref[...], s64_ref[...]
    inv_hd = 1.0 / HEAD_DIM

    def regroup(val, out_ref, dil, lane0, width):
        if dil == 1:
            out_ref[:, lane0:lane0 + 512] = val.astype(BF16)
            return
        for j, s_j in enumerate(scr):
            s_j[...] = val[:, j * LANES:(j + 1) * LANES]
        for r in range(dil):
            rows_r = jnp.concatenate([s_j[pl.ds(r, tm // dil, stride=dil), :] for s_j in scr], axis=1)
            out_ref[:, r * width + lane0:r * width + lane0 + 512] = rows_r.astype(BF16)

    for g in range(3):
        proj = _dot(xn, w_ref[:, g * 1536:(g + 1) * 1536])
        q, k, v = proj[:, 0:512], proj[:, 512:1024], proj[:, 1024:1536]
        q_n = q * lax.rsqrt(_seg_sum(q * q, seg) * inv_hd + EPS) * qg_ref[g:g + 1, :]
        q_r = _rope(q_n, c64, s64, HEAD_DIM // 2) * ATTN_SCALE
        k_n = k * lax.rsqrt(_seg_sum(k * k, seg) * inv_hd + EPS) * kg_ref[g:g + 1, :]
        k_r = _rope(k_n, c64, s64, HEAD_DIM // 2)
        kv_refs[g][:, 0:512] = k_r
        kv_refs[g][:, 512:1024] = v
        if dils is None:
            q_ref[:, g * 512:(g + 1) * 512] = q_r.astype(BF16)
        else:
            regroup(q_r, q_refs[g], dils[g], 0, 512)
            regroup(k_r, kb_refs[g], dils[g], 0, 1024)
            regroup(v, kb_refs[g], dils[g], 512, 1024)


def _odd_proj(x, tabs, wts, dils=None, tm=256):
    t, d = x.shape
    tm = min(tm, t)
    c64, s64 = tabs
    ntab = c64.shape[0] // tm
    const = lambda a: pl.BlockSpec(a.shape, lambda i: (0,) * a.ndim)
    tab = lambda a: pl.BlockSpec((tm, a.shape[1]), lambda i: (i % ntab, 0))
    row = lambda n, dil=1: pl.BlockSpec((tm // dil, n * dil), lambda i: (i, 0))
    shp = lambda n, dt, dil=1: jax.ShapeDtypeStruct((t // dil, n * dil), dt)
    consts = [wts[n] for n in ("g", "w_in", "seg", "qg", "kg")]
    if dils is None:
        out_shape = (shp(1536, BF16),) + (shp(1024, F32),) * 3
        out_specs = (row(1536),) + (row(1024),) * 3
        scratch = []
    else:
        assert all(tm % dl == 0 for dl in dils)
        out_shape = tuple(shp(512, BF16, dl) for dl in dils) + (shp(1024, F32),) * 3 + tuple(shp(1024, BF16, dl) for dl in dils)
        out_specs = tuple(row(512, dl) for dl in dils) + (row(1024),) * 3 + tuple(row(1024, dl) for dl in dils)
        scratch = [pltpu.VMEM((tm, LANES), F32)] * (512 // LANES)
    return pl.pallas_call(
        functools.partial(_odd_proj_kernel, dils=dils),
        out_shape=out_shape,
        grid=(t // tm,),
        in_specs=[row(d)] + [const(a) for a in consts] + [tab(c64), tab(s64)],
        out_specs=out_specs,
        scratch_shapes=scratch,
        compiler_params=_cparams(("parallel",), 48),
        name="odd_proj",
    )(x, *consts, c64, s64)


def _dil_prompt_kernel(q_ref, kp_ref, kc_ref, vp_ref, vc_ref, o_ref, lse_ref):
    i = pl.program_id(2)
    t = q_ref.shape[1]
    row = lax.broadcasted_iota(jnp.int32, (t, 2 * t), 0)
    col = lax.broadcasted_iota(jnp.int32, (t, 2 * t), 1)
    allowed = (col >= row) & (col <= row + t) & ((col >= t) | (i > 0))
    q = q_ref[0]
    k = jnp.concatenate([kp_ref[0], kc_ref[0]], axis=0)
    v = jnp.concatenate([vp_ref[0], vc_ref[0]], axis=0)
    for h in range(DIL_HEADS):
        hs = slice(h * HEAD_DIM, (h + 1) * HEAD_DIM)
        s = jnp.where(allowed, _nt(q[:, hs], k[:, hs]), -jnp.inf)
        m = jnp.max(s, axis=-1, keepdims=True)
        p = jnp.exp(s - m)
        l = jnp.sum(p, axis=-1, keepdims=True)
        o_ref[0, :, hs] = _dot(p.astype(BF16), v[:, hs]) / l
        lse_ref[0, :, hs] = jnp.broadcast_to(m + jnp.log(l), (t, HEAD_DIM))


def _dil_prompt(q, kvb, g, dil, b, l):
    t = DIL_TILE
    ls = l // dil
    ni = ls // t
    assert l % dil == 0 and ls % t == 0
    qv = q.reshape(b, ls, dil * 512)
    kv = kvb.reshape(b, ls, dil * 1024)
    blk = lambda f: pl.BlockSpec((1, t, 512), f)
    out = pl.pallas_call(
        _dil_prompt_kernel,
        out_shape=(jax.ShapeDtypeStruct((b, ls, dil * 512), F32),) * 2,
        grid=(b, dil, ni),
        in_specs=[
            blk(lambda bi, r, i: (bi, i, r)),
            blk(lambda bi, r, i: (bi, jnp.maximum(i - 1, 0), r * 2)),
            blk(lambda bi, r, i: (bi, i, r * 2)),
            blk(lambda bi, r, i: (bi, jnp.maximum(i - 1, 0), r * 2 + 1)),
            blk(lambda bi, r, i: (bi, i, r * 2 + 1)),
        ],
        out_specs=(blk(lambda bi, r, i: (bi, i, r)),) * 2,
        compiler_params=_cparams(("parallel", "parallel", "arbitrary"), 32),
        name=f"dil_prompt_{g}",
    )(qv, kv, kv, kv, kv)
    return out[0].reshape(b * ls, dil * 512), out[1].reshape(b * ls, dil * 512)


def _dil_sample_kernel(*refs, dil, n_tok, aliased):
    if aliased:
        qbd_ref, new_ref, st_ref, _, nst_ref, o_ref, lse_ref, newpad = refs
    else:
        qbd_ref, new_ref, st_ref, nst_ref, o_ref, lse_ref, newpad = refs
    feat, w = st_ref.shape
    half = feat // 2
    nq = qbd_ref.shape[1]
    heads = nq // n_tok
    newpad[...] = jnp.zeros_like(newpad)
    newpad[0:n_tok, :] = new_ref[0]
    npd = newpad[...]

    new_tail = pltpu.roll(npd.T, LANES - n_tok, 1)
    tail_lane = lax.broadcasted_iota(jnp.int32, (LANES, LANES), 1) >= LANES - n_tok
    for r0 in range(0, feat, LANES):
        shifted = pltpu.roll(st_ref[r0:r0 + LANES, :], w - n_tok, 1)
        nst_ref[r0:r0 + LANES, :] = shifted
        nst_ref[r0:r0 + LANES, w - LANES:w] = jnp.where(tail_lane, new_tail[r0:r0 + LANES, :], shifted[:, w - LANES:w])

    q = qbd_ref[0]
    kb = st_ref[0:half, :].astype(BF16)
    vb = st_ref[half:feat, :].astype(BF16)
    tok = lax.broadcasted_iota(jnp.int32, (nq, w), 0) // heads
    r = lax.broadcasted_iota(jnp.int32, (nq, w), 1)
    ok = (r >= tok) & (((w + tok - r) & (dil - 1)) == 0)
    s = jnp.where(ok, _dot(q, kb), -jnp.inf)

    tok2 = lax.broadcasted_iota(jnp.int32, (nq, LANES), 0) // heads
    j = lax.broadcasted_iota(jnp.int32, (nq, LANES), 1)
    ok2 = (j <= tok2) & (j < n_tok) & (((tok2 - j) & (dil - 1)) == 0)
    s2 = jnp.where(ok2, _nt(q, npd[:, 0:half].astype(BF16)), -jnp.inf)

    m = jnp.maximum(jnp.max(s, axis=-1, keepdims=True), jnp.max(s2, axis=-1, keepdims=True))
    p = jnp.exp(s - m)
    p2 = jnp.exp(s2 - m)
    l = jnp.sum(p, axis=-1, keepdims=True) + jnp.sum(p2, axis=-1, keepdims=True)
    o = (_nt(p.astype(BF16), vb) + _dot(p2.astype(BF16), npd[:, half:feat].astype(BF16))) / l
    head = lax.broadcasted_iota(jnp.int32, (nq, HEAD_DIM), 0) % heads
    o64 = jnp.zeros((nq, HEAD_DIM), F32)
    for h in range(DIL_HEADS):
        o64 = o64 + jnp.where(head == h, o[:, h * HEAD_DIM:(h + 1) * HEAD_DIM], 0.0)
    o_ref[0] = o64
    lse_ref[0] = jnp.broadcast_to(m + jnp.log(l), (nq, HEAD_DIM))


def _dil_sample(qbd, new, state, prev_out, li, dil):
    n_odd, b, feat, w = state.shape
    nq = qbd.shape[1]
    n_tok = new.shape[1]
    aliased = prev_out is not None
    in_specs = [
        pl.BlockSpec((1, nq, feat // 2), lambda i: (i, 0, 0)),
        pl.BlockSpec((1, n_tok, feat), lambda i: (i, 0, 0)),
        pl.BlockSpec((None, None, feat, w), lambda i: (li, i, 0, 0)),
    ]
    args = [qbd, new, state]
    if aliased:
        in_specs.append(pl.BlockSpec(memory_space=pl.ANY))
        args.append(prev_out)
    return pl.pallas_call(
        functools.partial(_dil_sample_kernel, dil=dil, n_tok=n_tok, aliased=aliased),
        out_shape=(
            jax.ShapeDtypeStruct(state.shape, F32),
            jax.ShapeDtypeStruct((b, nq, HEAD_DIM), F32),
            jax.ShapeDtypeStruct((b, nq, HEAD_DIM), F32),
        ),
        grid=(b,),
        in_specs=in_specs,
        out_specs=(
            pl.BlockSpec((None, None, feat, w), lambda i: (li, i, 0, 0)),
            pl.BlockSpec((1, nq, HEAD_DIM), lambda i: (i, 0, 0)),
            pl.BlockSpec((1, nq, HEAD_DIM), lambda i: (i, 0, 0)),
        ),
        scratch_shapes=[pltpu.VMEM((LANES, feat), F32)],
        input_output_aliases={3: 0} if aliased else {},
        compiler_params=_cparams(("parallel",), 56),
        name=f"dil_sample_{dil}",
    )(*args)


def _odd_out_kernel(*refs, dils):
    x_ref, o_refs, l_refs, wo_ref, out_ref, scr = refs[0], refs[1:4], refs[4:7], refs[7], refs[8], refs[9:]
    tm = x_ref.shape[0]
    free = list(scr)

    def tokens(ref, dil):
        if dil == 1:
            return ref[...]
        cols = [free.pop() for _ in range(512 // LANES)]
        for r in range(dil):
            for j, buf in enumerate(cols):
                buf[pl.ds(r, tm // dil, stride=dil), :] = ref[:, r * 512 + j * LANES:r * 512 + (j + 1) * LANES]
        return jnp.concatenate([buf[...] for buf in cols], axis=1)

    l0, l1, l2 = (tokens(ref, dl) for ref, dl in zip(l_refs, dils))
    o0, o1, o2 = (tokens(ref, dl) for ref, dl in zip(o_refs, dils))
    m = jnp.maximum(jnp.maximum(l0, l1), l2)
    e0, e1, e2 = jnp.exp(l0 - m), jnp.exp(l1 - m), jnp.exp(l2 - m)
    o = (e0 * o0 + e1 * o1 + e2 * o2) / (e0 + e1 + e2)
    out_ref[...] = x_ref[...] + _dot(o.astype(BF16), wo_ref[...])


def _odd_out(x, outs, lses, w_out, dils=(1, 1, 1), tm=512):
    t, d = x.shape
    tm = min(tm, t)
    assert all(tm % dl == 0 for dl in dils)
    row = lambda n, dil=1: pl.BlockSpec((tm // dil, n * dil), lambda i: (i, 0))
    grouped = [row(512, dl) for dl in dils]
    n_scr = 2 * sum(dl > 1 for dl in dils)
    return pl.pallas_call(
        functools.partial(_odd_out_kernel, dils=dils),
        out_shape=jax.ShapeDtypeStruct((t, d), F32),
        grid=(t // tm,),
        in_specs=[row(d)] + grouped + grouped + [pl.BlockSpec(w_out.shape, lambda i: (0, 0))],
        out_specs=row(d),
        scratch_shapes=[pltpu.VMEM((tm, LANES), F32)] * (n_scr * (512 // LANES)),
        compiler_params=_cparams(("parallel",), 40),
        name="odd_out",
    )(x, *outs, *lses, w_out)


def _rope_tables(pos, dim, reps):
    half = dim // 2
    inv = ROPE_THETA ** (-jnp.arange(half, dtype=F32) / half)
    ang = pos.astype(F32)[:, None] * inv[None, :]
    c, s = jnp.cos(ang), jnp.sin(ang)
    return jnp.tile(jnp.concatenate([c, c], axis=-1), (1, reps)), jnp.tile(jnp.concatenate([-s, s], axis=-1), (1, reps))


def _seg_matrix(group_of_lane):
    g = np.asarray(group_of_lane)
    return jnp.asarray(g[:, None] == g[None, :], BF16)


def _block_diag_q(q, n_heads, lanes_of_head):
    b, t, _ = q.shape
    n_groups = max(lanes_of_head) + 1
    onehot = jnp.asarray(np.asarray(lanes_of_head)[:, None] == np.arange(n_groups)[None, :], q.dtype)
    qh = q.reshape(b, t, n_heads, 1, HEAD_DIM) * onehot[None, None, :, :, None]
    return qh.reshape(b, t * n_heads, n_groups * HEAD_DIM)


def kernel(x_prompt, x_sample, cache_moba_kv, cache_mla, state_dil128, state_dil512, state_dil2048, page_table,
           norm_g, ffn_w_in, ffn_w_out, even_w_in, moba_q_g, moba_k_g, mla_cq_g, mla_w_uq, mla_q_g, mla_ckv_g,
           mla_kr_g, mla_w_uk, mla_w_uv, even_w_out, odd_w_in, dil_q_g, dil_k_g, odd_w_out):
    bp, sp, d = x_prompt.shape
    bs, ts, _ = x_sample.shape
    depth = norm_g.shape[0]
    n_even, n_pool, page = cache_moba_kv.shape[:3]
    past = page_table.shape[1] * page
    pos_p = jnp.arange(sp, dtype=jnp.int32)
    pos_s = past + jnp.arange(ts, dtype=jnp.int32)
    tm_s = min(256, bs * ts)
    pos_s_tile = jnp.tile(pos_s, tm_s // ts)
    tab64_p, tab64_s = _rope_tables(pos_p, HEAD_DIM, 8), _rope_tables(pos_s_tile, HEAD_DIM, 8)
    tab32_p, tab32_s = _rope_tables(pos_p, MLA_ROPE, 8), _rope_tables(pos_s_tile, MLA_ROPE, 8)

    seg64 = _seg_matrix(np.arange(512) // HEAD_DIM)
    qdim = MLA_NOPE + MLA_ROPE
    perm = np.concatenate([np.concatenate([np.arange(h * qdim, h * qdim + MLA_NOPE) for h in range(MLA_HEADS)]),
                           np.concatenate([np.arange(h * qdim + MLA_NOPE, (h + 1) * qdim) for h in range(MLA_HEADS)])])
    m96 = _seg_matrix(perm // qdim)
    place = jnp.asarray((np.arange(256)[:, None] // MLA_ROPE == np.arange(1024)[None, :] // LANES)
                        & (np.arange(256)[:, None] % MLA_ROPE == np.arange(1024)[None, :] % LANES), BF16)
    eye_h = np.eye(MLA_HEADS, dtype=np.float32)

    cache_moba = jnp.transpose(cache_moba_kv, (0, 1, 3, 4, 5, 2)).reshape(
        n_even, n_pool, 2 * MOBA_KV_HEADS * HEAD_DIM, page)
    cache_mla_t = jnp.transpose(cache_mla, (0, 1, 3, 2))
    states = [jnp.transpose(s, (0, 1, 3, 4, 5, 2)).reshape(s.shape[0], s.shape[1], 2 * DIL_HEADS * HEAD_DIM, s.shape[2])
              for s in (state_dil128, state_dil512, state_dil2048)]
    new_states = [None, None, None]

    xp = x_prompt.reshape(bp * sp, d)
    xs = x_sample.reshape(bs * ts, d)
    moba_p, moba_s, mla_p, mla_s = [], [], [], []
    dil_p = [[] for _ in DIL_CONFIGS]

    for layer in range(depth):
        li = layer // 2
        w_in = ffn_w_in[layer, 0].astype(BF16)
        w_out = ffn_w_out[layer, 0].astype(BF16)
        xp = _ffn(xp, norm_g[layer, 0], w_in, w_out)
        xs = _ffn(xs, norm_g[layer, 0], w_in, w_out)
        if layer % 2 == 0:
            wts = {
                "g": norm_g[layer, 1].reshape(1, d),
                "w_in": jnp.pad(even_w_in[li], ((0, 0), (0, 1664 - even_w_in.shape[2]))).astype(BF16),
                "seg": seg64,
                "m96": m96,
                "wuq": mla_w_uq[li][:, perm].astype(BF16),
                "wuk": (jnp.transpose(mla_w_uk[li], (1, 2, 0))[:, :, None, :] * eye_h[:, None, :, None]
                        ).reshape(MLA_HEADS * MLA_NOPE, MLA_HEADS * MLA_KV_LORA).astype(BF16),
                "place": place,
                "qg": jnp.tile(moba_q_g[li], MOBA_HEADS).reshape(1, 512),
                "kg": jnp.tile(moba_k_g[li], MOBA_KV_HEADS).reshape(1, 256),
                "cqg": mla_cq_g[li].reshape(1, MLA_Q_LORA),
                "mqg": jnp.tile(mla_q_g[li], MLA_HEADS)[perm].reshape(1, MLA_HEADS * qdim),
                "ckvg": mla_ckv_g[li].reshape(1, MLA_KV_LORA),
                "krg": jnp.pad(mla_kr_g[li], (0, LANES - MLA_ROPE)).reshape(1, LANES),
            }
            wuv_bd = (jnp.transpose(mla_w_uv[li], (1, 0, 2))[:, :, None, :] * eye_h[:, None, :, None]
                      ).reshape(MLA_HEADS * MLA_KV_LORA, MLA_HEADS * MLA_V).astype(BF16)
            w_o = even_w_out[li].astype(BF16)
            qa, kv, lat, qf, latk = _even_proj(xp, tab64_p + tab32_p, wts)
            oa = _moba_prompt(qa, kv, bp, sp)
            ol = _mla_prompt(qf, latk, bp, sp)
            xp = _even_out(xp, oa, ol, wuv_bd, w_o)
            moba_p.append(kv.reshape(bp, sp, 2, MOBA_KV_HEADS, HEAD_DIM))
            mla_p.append(lat.reshape(bp, sp, MLA_KV_LORA + MLA_ROPE))
            qa, kv, lat, qf, latk = _even_proj(xs, tab64_s + tab32_s, wts)
            kvh_of_head = [h // (MOBA_HEADS // MOBA_KV_HEADS) for h in range(MOBA_HEADS)]
            qbd = _block_diag_q(qa.reshape(bs, ts, 512), MOBA_HEADS, kvh_of_head)
            oa, ol = _even_sample(page_table, qbd, kv.reshape(bs, ts, 512), qf.reshape(bs, ts * MLA_HEADS, MLA_QW),
                                  latk.astype(F32).reshape(bs, ts, MLA_QW), cache_moba, cache_mla_t, li)
            xs = _even_out(xs, oa.reshape(bs * ts, 512), ol.reshape(bs * ts, MLA_HEADS * MLA_KV_LORA), wuv_bd, w_o)
            moba_s.append(kv.reshape(bs, ts, 2, MOBA_KV_HEADS, HEAD_DIM))
            mla_s.append(lat.reshape(bs, ts, MLA_KV_LORA + MLA_ROPE))
        else:
            wts = {
                "g": norm_g[layer, 1].reshape(1, d),
                "w_in": odd_w_in[li].astype(BF16),
                "seg": seg64,
                "qg": jnp.tile(dil_q_g[li], (1, DIL_HEADS)),
                "kg": jnp.tile(dil_k_g[li], (1, DIL_HEADS)),
            }
            w_o = odd_w_out[li].astype(BF16)
            dils = tuple(dil for _, dil in DIL_CONFIGS)
            q0, q1, q2, kv0, kv1, kv2, kb0, kb1, kb2 = _odd_proj(xp, tab64_p, wts, dils=dils)
            outs, lses = [], []
            for g, ((w, dil), qg, kvf, kvb) in enumerate(zip(DIL_CONFIGS, (q0, q1, q2), (kv0, kv1, kv2), (kb0, kb1, kb2))):
                o, lse = _dil_prompt(qg, kvb, g, dil, bp, sp)
                outs.append(o)
                lses.append(lse)
                wp = min(w, sp)
                dil_p[g].append(kvf.reshape(bp, sp, 2, DIL_HEADS, HEAD_DIM)[:, sp - wp:])
            xp = _odd_out(xp, outs, lses, w_o, dils=dils)
            q, kv0, kv1, kv2 = _odd_proj(xs, tab64_s, wts)
            outs, lses = [], []
            for g, ((w, dil), kvf) in enumerate(zip(DIL_CONFIGS, (kv0, kv1, kv2))):
                qbd = _block_diag_q(q[:, g * 512:(g + 1) * 512].reshape(bs, ts, 512), DIL_HEADS, list(range(DIL_HEADS)))
                nst, o, lse = _dil_sample(qbd, kvf.reshape(bs, ts, 1024), states[g], new_states[g], li, dil)
                new_states[g] = nst
                outs.append(o.reshape(bs * ts, 512))
                lses.append(lse.reshape(bs * ts, 512))
            xs = _odd_out(xs, outs, lses, w_o)
        w_in = ffn_w_in[layer, 1].astype(BF16)
        w_out = ffn_w_out[layer, 1].astype(BF16)
        xp = _ffn(xp, norm_g[layer, 2], w_in, w_out)
        xs = _ffn(xs, norm_g[layer, 2], w_in, w_out)

    dil_s = [jnp.transpose(ns.reshape(ns.shape[0], ns.shape[1], 2, DIL_HEADS, HEAD_DIM, ns.shape[3]), (0, 1, 5, 2, 3, 4))
             for ns in new_states]
    return (xp.reshape(bp, sp, d), xs.reshape(bs, ts, d),
            jnp.stack(moba_p, axis=0), jnp.stack(moba_s, axis=0),
            jnp.stack(mla_p, axis=0), jnp.stack(mla_s, axis=0),
            jnp.stack(dil_p[0], axis=0), dil_s[0],
            jnp.stack(dil_p[1], axis=0), dil_s[1],
            jnp.stack(dil_p[2], axis=0), dil_s[2])
```
